```python
import jax, jax.numpy as jnp
from jax import lax
import numpy as np

D_MODEL = 2048
BATCH = 2
SEQ = 4096
DEPTH = 1

CONV_WIDTH = 1024
CONV_K = 3
N_HEADS = 8
HEAD_DIM = 128
ATTN_WIDTH = N_HEADS * HEAD_DIM
ROT_DIM = HEAD_DIM // 4
ROPE_THETA = 500000.0
MOBA_BLOCK = 256
MOBA_TOPK = 3
Q_CHUNK = 32
PEER_HEADS = 8
PEER_NKEYS = 128
PEER_N_EXPERTS = PEER_NKEYS * PEER_NKEYS
PEER_KEY_DIM = 256
PEER_HALF = PEER_KEY_DIM // 2
PEER_TOPK = 16
TOK_CHUNK = 128
RMS_EPS = 1e-6
IN_SIZES = (CONV_WIDTH, CONV_WIDTH, CONV_WIDTH, ATTN_WIDTH, ATTN_WIDTH, ATTN_WIDTH, D_MODEL, D_MODEL)
IN_WIDTH = sum(IN_SIZES)
IN_SPLITS = tuple(int(s) for s in np.cumsum(IN_SIZES)[:-1])

kernel_name = "hybrid_conv_moba_peer_block"


def rms_norm(x, g):
    x32 = x.astype(jnp.float32)
    y = x32 * lax.rsqrt(jnp.mean(x32 * x32, axis=-1, keepdims=True) + RMS_EPS)
    return y.astype(x.dtype) * g


def partial_rotary(x, positions):
    half = ROT_DIM // 2
    inv_freq = 1.0 / (ROPE_THETA ** (jnp.arange(0, ROT_DIM, 2, dtype=jnp.float32) / ROT_DIM))
    ang = positions.astype(jnp.float32)[:, :, None] * inv_freq
    cos = jnp.cos(ang)[:, :, None, :]
    sin = jnp.sin(ang)[:, :, None, :]
    x32 = x.astype(jnp.float32)
    x1 = x32[..., :half]
    x2 = x32[..., half:ROT_DIM]
    rot = jnp.concatenate([x1 * cos - x2 * sin, x2 * cos + x1 * sin], axis=-1)
    return jnp.concatenate([rot.astype(x.dtype), x[..., ROT_DIM:]], axis=-1)


def short_conv_mixer(b_gate, c_gate, xc, conv_w):
    z = c_gate * xc
    s = z.shape[1]
    zp = jnp.pad(z, ((0, 0), (CONV_K - 1, 0), (0, 0)))
    conv = sum(conv_w[j] * zp[:, j:j + s] for j in range(CONV_K))
    return b_gate * conv


def moba_attention(q, k, v):
    b, s, h, hd = q.shape
    bh = b * h
    nb = -(-s // MOBA_BLOCK)
    s_pad = nb * MOBA_BLOCK
    n_sel = min(MOBA_TOPK, nb)
    scale = hd ** -0.5
    qh = q.transpose(0, 2, 1, 3).reshape(bh, s, hd)
    kp = jnp.pad(k.transpose(0, 2, 1, 3).reshape(bh, s, hd), ((0, 0), (0, s_pad - s), (0, 0)))
    vp = jnp.pad(v.transpose(0, 2, 1, 3).reshape(bh, s, hd), ((0, 0), (0, s_pad - s), (0, 0)))
    kb = kp.reshape(bh, nb, MOBA_BLOCK, hd)
    vb = vp.reshape(bh, nb, MOBA_BLOCK, hd)
    kmean = jnp.mean(kb.astype(jnp.float32), axis=2)

    def one_chunk(c):
        q0 = c * Q_CHUNK
        blk = q0 // MOBA_BLOCK
        qc = lax.dynamic_slice_in_dim(qh, q0, Q_CHUNK, axis=1)
        gate = jnp.einsum('nqd,nbd->nqb', qc.astype(jnp.float32), kmean)
        gate = jnp.where(jnp.arange(nb)[None, None, :] < blk, gate, -jnp.inf)
        _, top_i = lax.top_k(gate, n_sel)
        valid = jnp.arange(n_sel) < blk
        k_sel = jax.vmap(lambda kbn, idx: kbn[idx])(kb, top_i)
        v_sel = jax.vmap(lambda vbn, idx: vbn[idx])(vb, top_i)
        s_sel = jnp.einsum('nqd,nqjkd->nqjk', qc, k_sel).astype(jnp.float32) * scale
        s_sel = jnp.where(valid[None, None, :, None], s_sel, -jnp.inf)
        s_sel = s_sel.reshape(bh, Q_CHUNK, n_sel * MOBA_BLOCK)
        k_own = lax.dynamic_slice_in_dim(kp, blk * MOBA_BLOCK, MOBA_BLOCK, axis=1)
        v_own = lax.dynamic_slice_in_dim(vp, blk * MOBA_BLOCK, MOBA_BLOCK, axis=1)
        s_own = jnp.einsum('nqd,nkd->nqk', qc, k_own).astype(jnp.float32) * scale
        q_pos = q0 + jnp.arange(Q_CHUNK)
        k_pos = blk * MOBA_BLOCK + jnp.arange(MOBA_BLOCK)
        s_own = jnp.where(k_pos[None, None, :] <= q_pos[None, :, None], s_own, -jnp.inf)
        p = jax.nn.softmax(jnp.concatenate([s_sel, s_own], axis=-1), axis=-1)
        p_sel = p[..., :n_sel * MOBA_BLOCK].reshape(bh, Q_CHUNK, n_sel, MOBA_BLOCK).astype(v.dtype)
        p_own = p[..., n_sel * MOBA_BLOCK:].astype(v.dtype)
        return (jnp.einsum('nqjk,nqjkd->nqd', p_sel, v_sel)
                + jnp.einsum('nqk,nkd->nqd', p_own, v_own))

    out = lax.map(one_chunk, jnp.arange(s // Q_CHUNK))
    out = out.transpose(1, 0, 2, 3).reshape(b, h, s, hd)
    return out.transpose(0, 2, 1, 3).reshape(b, s, h * hd)


def peer_ffn(xn, w_pq, sub_keys, expert_u, expert_v):
    b, s, d = xn.shape
    t = b * s
    xt = xn.reshape(t, d)
    qp = (xt @ w_pq).reshape(t, PEER_HEADS, 2, PEER_HALF)
    sc = jnp.einsum('thpd,hpnd->thpn', qp, sub_keys).astype(jnp.float32)
    s_top, i_top = lax.top_k(sc, PEER_TOPK)
    cand = s_top[:, :, 0, :, None] + s_top[:, :, 1, None, :]
    cand_idx = i_top[:, :, 0, :, None] * PEER_NKEYS + i_top[:, :, 1, None, :]
    best_s, best_pos = lax.top_k(cand.reshape(t, PEER_HEADS, PEER_TOPK * PEER_TOPK), PEER_TOPK)
    experts = jnp.take_along_axis(cand_idx.reshape(t, PEER_HEADS, PEER_TOPK * PEER_TOPK), best_pos, axis=-1)
    gates = jax.nn.softmax(best_s, axis=-1)
    n_chunks = t // TOK_CHUNK

    def one_chunk(args):
        xc, idx, g = args
        u = expert_u[idx]
        v = expert_v[idx]
        a = jnp.einsum('thkd,td->thk', u, xc)
        act = (jax.nn.gelu(a.astype(jnp.float32)) * g).astype(xc.dtype)
        return jnp.einsum('thk,thkd->td', act, v)

    y = lax.map(one_chunk, (xt.reshape(n_chunks, TOK_CHUNK, d),
                            experts.reshape(n_chunks, TOK_CHUNK, PEER_HEADS, PEER_TOPK),
                            gates.reshape(n_chunks, TOK_CHUNK, PEER_HEADS, PEER_TOPK)))
    return y.reshape(b, s, d)


def setup_inputs(seed: int = 0) -> dict:
    key = jax.random.key(seed)
    ks = jax.random.split(key, 16)
    f32 = jnp.float32
    nrm = lambda k, shape, sc: jax.random.normal(k, shape, f32) * sc
    x = nrm(ks[0], (BATCH, SEQ, D_MODEL), 1.0)
    offset = jax.random.randint(ks[1], (BATCH, 1), 0, 4096, dtype=jnp.int32)
    positions = offset + jnp.arange(SEQ, dtype=jnp.int32)[None, :]
    return {
        "x": x,
        "positions": positions,
        "attn_norm_g": 1.0 + nrm(ks[2], (DEPTH, D_MODEL), 0.02),
        "w_in": nrm(ks[3], (DEPTH, D_MODEL, IN_WIDTH), D_MODEL ** -0.5),
        "gate_bias": nrm(ks[4], (DEPTH, 2 * D_MODEL), 0.01),
        "conv_w": nrm(ks[5], (DEPTH, CONV_K, CONV_WIDTH), CONV_K ** -0.5),
        "w_branch_conv": nrm(ks[6], (DEPTH, CONV_WIDTH, D_MODEL), CONV_WIDTH ** -0.5),
        "w_branch_attn": nrm(ks[7], (DEPTH, ATTN_WIDTH, D_MODEL), ATTN_WIDTH ** -0.5),
        "w_out": nrm(ks[8], (DEPTH, D_MODEL, D_MODEL), D_MODEL ** -0.5),
        "ffn_norm_g": 1.0 + nrm(ks[9], (DEPTH, D_MODEL), 0.02),
        "w_peer_query": nrm(ks[10], (DEPTH, D_MODEL, PEER_HEADS * PEER_KEY_DIM), D_MODEL ** -0.5),
        "peer_sub_keys": nrm(ks[11], (DEPTH, PEER_HEADS, 2, PEER_NKEYS, PEER_HALF), PEER_HALF ** -0.5),
        "peer_u": nrm(ks[12], (DEPTH, PEER_N_EXPERTS, D_MODEL), D_MODEL ** -0.5),
        "peer_v": nrm(ks[13], (DEPTH, PEER_N_EXPERTS, D_MODEL), PEER_HEADS ** -0.5),
        "final_norm_g": 1.0 + nrm(ks[14], (D_MODEL,), 0.02),
    }


def reference(x, positions, attn_norm_g, w_in, gate_bias, conv_w, w_branch_conv, w_branch_attn,
              w_out, ffn_norm_g, w_peer_query, peer_sub_keys, peer_u, peer_v, final_norm_g):
    b, s, _ = x.shape
    for l in range(DEPTH):
        h = rms_norm(x, attn_norm_g[l])
        proj = h @ w_in[l]
        cb, cc, cx, q, k, v, g_conv, g_attn = jnp.split(proj, IN_SPLITS, axis=-1)
        y_conv = short_conv_mixer(cb, cc, cx, conv_w[l]) @ w_branch_conv[l]
        q = partial_rotary(q.reshape(b, s, N_HEADS, HEAD_DIM), positions)
        k = partial_rotary(k.reshape(b, s, N_HEADS, HEAD_DIM), positions)
        v = v.reshape(b, s, N_HEADS, HEAD_DIM)
        y_attn = moba_attention(q, k, v) @ w_branch_attn[l]
        gb = gate_bias[l]
        gate_a = jax.nn.sigmoid(g_conv + gb[:D_MODEL])
        gate_b = jax.nn.sigmoid(g_attn + gb[D_MODEL:])
        x = x + (gate_a * y_conv + gate_b * y_attn) @ w_out[l]
        x = x + peer_ffn(rms_norm(x, ffn_norm_g[l]), w_peer_query[l], peer_sub_keys[l], peer_u[l], peer_v[l])
    return rms_norm(x, final_norm_g)
```

```python
import functools

import jax
import jax.numpy as jnp
from jax import lax
from jax.experimental import pallas as pl
from jax.experimental.pallas import tpu as pltpu

F32 = jnp.float32
BF16 = jnp.bfloat16

HEAD_DIM = 128
ROT_DIM = HEAD_DIM // 4
ROPE_THETA = 500000.0
MOBA_BLOCK = 256
MOBA_TOPK = 3
PEER_TOPK = 16
RMS_EPS = 1e-6

LANES = 128
SUBLANES = 8
VMEM_LIMIT_BYTES = 56 * 1024 * 1024

NEG_BIG = -1e30


def _params(*sem):
    return pltpu.CompilerParams(dimension_semantics=sem, vmem_limit_bytes=VMEM_LIMIT_BYTES)


def _resident(shape):
    zeros = (0,) * len(shape)
    return pl.BlockSpec(shape, lambda *_: zeros, pipeline_mode=pl.Buffered(1))


def _rms_norm(x, g):
    ms = jnp.mean(x * x, axis=-1, keepdims=True)
    return x * lax.rsqrt(ms + RMS_EPS) * g


def _inproj_kernel(x_ref, g_ref, w_ref, o_ref, h_ref):
    @pl.when(pl.program_id(1) == 0)
    def _():
        h_ref[...] = _rms_norm(x_ref[...], g_ref[...]).astype(BF16)

    o_ref[...] = jnp.dot(h_ref[...], w_ref[...], preferred_element_type=F32)


def _inproj(x, g, w, tm, tn):
    t, d = x.shape
    n = w.shape[1]
    return pl.pallas_call(
        _inproj_kernel,
        out_shape=jax.ShapeDtypeStruct((t, n), F32),
        grid=(t // tm, n // tn),
        in_specs=[
            pl.BlockSpec((tm, d), lambda i, j: (i, 0)),
            pl.BlockSpec((1, d), lambda i, j: (0, 0)),
            pl.BlockSpec((d, tn), lambda i, j: (0, j)),
        ],
        out_specs=pl.BlockSpec((tm, tn), lambda i, j: (i, j)),
        scratch_shapes=[pltpu.VMEM((tm, d), BF16)],
        compiler_params=_params("parallel", "arbitrary"),
        name="inproj",
    )(x, g, w)


def _qkvprep_kernel(q_ref, k_ref, v_ref, pos_ref, invf_ref, qo_ref, ko_ref, vo_ref, km_ref, *, n_heads):
    half = ROT_DIM // 2
    ang = pos_ref[...].astype(F32) * invf_ref[...]
    cos = jnp.cos(ang)
    sin = jnp.sin(ang)
    lane = lax.broadcasted_iota(jnp.int32, ang.shape, 1)
    sin_lo = jnp.where(lane < half, -sin, 0.0)
    sin_hi = jnp.where((lane >= half) & (lane < ROT_DIM), sin, 0.0)

    def rot(x):
        return x * cos + pltpu.roll(x, HEAD_DIM - half, 1) * sin_lo + pltpu.roll(x, half, 1) * sin_hi

    for h in range(n_heads):
        sl = slice(h * HEAD_DIM, (h + 1) * HEAD_DIM)
        qo_ref[:, sl] = rot(q_ref[:, sl])
        kr = rot(k_ref[:, sl])
        ko_ref[:, sl] = kr.astype(BF16)
        km_ref[0, :, sl] = jnp.mean(kr, axis=0, keepdims=True)
    vo_ref[...] = v_ref[...].astype(BF16)


def _qkvprep(proj, pos, invf, q_col, n_heads):
    t = proj.shape[0]
    w = n_heads * HEAD_DIM
    r = MOBA_BLOCK
    qb = q_col // w
    col = lambda c: pl.BlockSpec((r, w), lambda i: (i, c))
    return pl.pallas_call(
        functools.partial(_qkvprep_kernel, n_heads=n_heads),
        out_shape=(
            jax.ShapeDtypeStruct((t, w), F32),
            jax.ShapeDtypeStruct((t, w), BF16),
            jax.ShapeDtypeStruct((t, w), BF16),
            jax.ShapeDtypeStruct((t // r, 1, w), F32),
        ),
        grid=(t // r,),
        in_specs=[col(qb), col(qb + 1), col(qb + 2),
                  pl.BlockSpec((r, 1), lambda i: (i, 0)),
                  pl.BlockSpec((1, HEAD_DIM), lambda i: (0, 0))],
        out_specs=(
            pl.BlockSpec((r, w), lambda i: (i, 0)),
            pl.BlockSpec((r, w), lambda i: (i, 0)),
            pl.BlockSpec((r, w), lambda i: (i, 0)),
            pl.BlockSpec((1, 1, w), lambda i: (i, 0, 0)),
        ),
        compiler_params=_params("parallel"),
        name="qkvprep",
    )(proj, proj, proj, pos, invf)


def _attn_kernel(q_ref, k_ref, v_ref, km_ref, o_ref, *, scale):
    i = pl.program_id(1)
    blk = MOBA_BLOCK
    q = q_ref[...]
    km = km_ref[0]
    nb = km.shape[0]

    gate = lax.dot_general(km, q, (((1,), (1,)), ((), ())),
                           precision=lax.Precision.HIGHEST, preferred_element_type=F32)
    bidx = lax.broadcasted_iota(jnp.int32, gate.shape, 0)
    gate = jnp.where(bidx < i, gate, -jnp.inf)
    sel = jnp.zeros(gate.shape, F32)
    for r in range(MOBA_TOPK):
        top = jnp.max(gate, axis=0, keepdims=True)
        first = jnp.min(jnp.where(gate == top, bidx, nb), axis=0, keepdims=True)
        hit = bidx == first
        sel = jnp.where(hit & (r < i), 1.0, sel)
        gate = jnp.where(hit, -jnp.inf, gate)
    bias_t = jnp.where(sel > 0.0, 0.0, NEG_BIG)
    bias_t = jnp.concatenate([bias_t, jnp.zeros((LANES - nb, blk), F32)], axis=0)
    bias = bias_t.T.astype(BF16)

    qb = q.astype(BF16)
    q_aug = jnp.concatenate([qb, bias], axis=1)
    nt = (((1,), (1,)), ((), ()))

    k_own = k_ref[pl.ds(pl.multiple_of(i * blk, blk), blk), :]
    v_own = v_ref[pl.ds(pl.multiple_of(i * blk, blk), blk), :]
    s = lax.dot_general(qb, k_own, nt, preferred_element_type=F32) * scale
    row = lax.broadcasted_iota(jnp.int32, s.shape, 0)
    colv = lax.broadcasted_iota(jnp.int32, s.shape, 1)
    s = jnp.where(colv <= row, s, -jnp.inf)
    m0 = jnp.max(s, axis=-1, keepdims=True)
    p = jnp.exp(s - m0)
    l0 = jnp.sum(p, axis=-1, keepdims=True)
    acc0 = jnp.dot(p.astype(BF16), v_own, preferred_element_type=F32)

    lane = lax.broadcasted_iota(jnp.int32, (blk, LANES), 1)

    def body(j, carry):
        m, l, acc = carry
        start = pl.multiple_of(j * blk, blk)
        k_j = k_ref[pl.ds(start, blk), :]
        v_j = v_ref[pl.ds(start, blk), :]
        onehot = jnp.where(lane == j, 1.0, 0.0).astype(BF16)
        k_aug = jnp.concatenate([k_j, onehot], axis=1)
        s = lax.dot_general(q_aug, k_aug, nt, preferred_element_type=F32) * scale
        m_new = jnp.maximum(m, jnp.max(s, axis=-1, keepdims=True))
        alpha = jnp.exp(m - m_new)
        p = jnp.exp(s - m_new)
        l = alpha * l + jnp.sum(p, axis=-1, keepdims=True)
        acc = alpha * acc + jnp.dot(p.astype(BF16), v_j, preferred_element_type=F32)
        return m_new, l, acc

    _, l, acc = lax.fori_loop(0, i, body, (m0, l0, acc0))
    o_ref[...] = (acc / l).astype(o_ref.dtype)


def _attn(q, k, v, kmean, batch, seq, n_heads):
    t, w = q.shape
    nq = seq // MOBA_BLOCK
    blk = MOBA_BLOCK
    return pl.pallas_call(
        functools.partial(_attn_kernel, scale=HEAD_DIM ** -0.5),
        out_shape=jax.ShapeDtypeStruct((t, w), BF16),
        grid=(batch * n_heads, nq),
        in_specs=[
            pl.BlockSpec((blk, HEAD_DIM), lambda bh, i: ((bh // n_heads) * nq + i, bh % n_heads)),
            pl.BlockSpec((seq, HEAD_DIM), lambda bh, i: (bh // n_heads, bh % n_heads)),
            pl.BlockSpec((seq, HEAD_DIM), lambda bh, i: (bh // n_heads, bh % n_heads)),
            pl.BlockSpec((1, nq, HEAD_DIM), lambda bh, i: (bh // n_heads, 0, bh % n_heads)),
        ],
        out_specs=pl.BlockSpec((blk, HEAD_DIM), lambda bh, i: ((bh // n_heads) * nq + i, bh % n_heads)),
        compiler_params=_params("parallel", "arbitrary"),
        name="attn",
    )(q, k, v, kmean)


def _merge_kernel(cb_ref, cc_ref, cx_ref, pc_ref, px_ref, ga_ref, gb_ref, at_ref, x_ref,
                  cw_ref, bias_ref, wa_ref, wb_ref, wo_ref, ng_ref,
                  x1_ref, xn_ref, z_ref, *, seq):
    tm = cb_ref.shape[0]
    d = x_ref.shape[1]
    first = (pl.program_id(0) * tm) % seq == 0
    z = cc_ref[...] * cx_ref[...]
    z_prev = jnp.where(first, 0.0, pc_ref[...] * px_ref[...])
    z_ref[0:SUBLANES, :] = z_prev
    z_ref[SUBLANES:, :] = z
    cw = cw_ref[...]
    conv = (cw[2:3, :] * z
            + cw[1:2, :] * z_ref[SUBLANES - 1:SUBLANES - 1 + tm, :]
            + cw[0:1, :] * z_ref[SUBLANES - 2:SUBLANES - 2 + tm, :])
    u = (cb_ref[...] * conv).astype(BF16)
    y_conv = jnp.dot(u, wa_ref[...], preferred_element_type=F32)
    y_attn = jnp.dot(at_ref[...], wb_ref[...], preferred_element_type=F32)
    bias = bias_ref[...]
    merged = (jax.nn.sigmoid(ga_ref[...] + bias[:, :d]) * y_conv
              + jax.nn.sigmoid(gb_ref[...] + bias[:, d:]) * y_attn)
    x1 = x_ref[...] + jnp.dot(merged.astype(BF16), wo_ref[...], preferred_element_type=F32)
    x1_ref[...] = x1
    xn_ref[...] = _rms_norm(x1, ng_ref[...]).astype(BF16)


def _merge(proj, attn, x, conv_w, gate_bias, wa, wb, wo, ng, seq, tm):
    t, d = x.shape
    cw = wa.shape[0]
    aw = wb.shape[0]
    gcol = (3 * cw + 3 * aw) // d
    halo = lambda c: pl.BlockSpec(
        (SUBLANES, cw), lambda i: (jnp.maximum(i * (tm // SUBLANES) - 1, 0), c))
    return pl.pallas_call(
        functools.partial(_merge_kernel, seq=seq),
        out_shape=(jax.ShapeDtypeStruct((t, d), F32), jax.ShapeDtypeStruct((t, d), BF16)),
        grid=(t // tm,),
        in_specs=[
            pl.BlockSpec((tm, cw), lambda i: (i, 0)),
            pl.BlockSpec((tm, cw), lambda i: (i, 1)),
            pl.BlockSpec((tm, cw), lambda i: (i, 2)),
            halo(1), halo(2),
            pl.BlockSpec((tm, d), lambda i: (i, gcol)),
            pl.BlockSpec((tm, d), lambda i: (i, gcol + 1)),
            pl.BlockSpec((tm, aw), lambda i: (i, 0)),
            pl.BlockSpec((tm, d), lambda i: (i, 0)),
            _resident(conv_w.shape), _resident(gate_bias.shape),
            _resident(wa.shape), _resident(wb.shape), _resident(wo.shape), _resident(ng.shape),
        ],
        out_specs=(pl.BlockSpec((tm, d), lambda i: (i, 0)), pl.BlockSpec((tm, d), lambda i: (i, 0))),
        scratch_shapes=[pltpu.VMEM((tm + SUBLANES, cw), F32)],
        compiler_params=_params("parallel"),
        name="merge",
    )(proj, proj, proj, proj, proj, proj, proj, attn, x, conv_w, gate_bias, wa, wb, wo, ng)


def _top_values(x, k):
    rows = x.shape[0]
    ridx = lax.broadcasted_iota(jnp.int32, x.shape, 0)
    oidx = lax.broadcasted_iota(jnp.int32, (k, x.shape[1]), 0)

    def body(r, carry):
        x, out = carry
        top = jnp.max(x, axis=0, keepdims=True)
        first = jnp.min(jnp.where(x == top, ridx, rows), axis=0, keepdims=True)
        x = jnp.where(ridx == first, -jnp.inf, x)
        out = jnp.where(oidx == r, top, out)
        return x, out

    _, out = lax.fori_loop(0, k, body, (x, jnp.zeros((k, x.shape[1]), F32)))
    return out


def _pscore_kernel(xn_ref, wq_ref, keys_ref, s1_ref, s2_ref, w1_ref, w2_ref, tau_ref, *, n_heads):
    nkeys, half = keys_ref.shape[1], keys_ref.shape[2]
    qp = jnp.dot(xn_ref[...], wq_ref[...], preferred_element_type=F32)
    nt = (((1,), (1,)), ((), ()))
    for h in range(n_heads):
        sc = []
        tops = []
        for p in range(2):
            c = (2 * h + p) * half
            s = lax.dot_general(keys_ref[2 * h + p], qp[:, c:c + half], nt,
                                precision=lax.Precision.HIGHEST, preferred_element_type=F32)
            sc.append(s)
            tops.append(_top_values(s, PEER_TOPK))
        a, b = tops
        cand = jnp.concatenate([a[p:p + 1, :] + b for p in range(PEER_TOPK)], axis=0)
        best = _top_values(cand, PEER_TOPK)
        z = jnp.sum(jnp.exp(best - best[0:1, :]), axis=0, keepdims=True)
        s1_ref[h] = sc[0]
        s2_ref[h] = sc[1]
        w1_ref[h] = jnp.exp(sc[0] - a[0:1, :]) / z
        w2_ref[h] = jnp.exp(sc[1] - b[0:1, :])
        tau_ref[h] = best[PEER_TOPK - 1:PEER_TOPK, :]


def _pscore(xn, wq, keys, n_heads, tm):
    t, d = xn.shape
    nkeys = keys.shape[1]
    big = jax.ShapeDtypeStruct((n_heads, nkeys, t), F32)
    bspec = pl.BlockSpec((n_heads, nkeys, tm), lambda i: (0, 0, i))
    return pl.pallas_call(
        functools.partial(_pscore_kernel, n_heads=n_heads),
        out_shape=(big, big, big, big, jax.ShapeDtypeStruct((n_heads, 1, t), F32)),
        grid=(t // tm,),
        in_specs=[pl.BlockSpec((tm, d), lambda i: (i, 0)), _resident(wq.shape), _resident(keys.shape)],
        out_specs=(bspec, bspec, bspec, bspec, pl.BlockSpec((n_heads, 1, tm), lambda i: (0, 0, i))),
        compiler_params=_params("parallel"),
        name="pscore",
    )(xn, wq, keys)


def _pdense_kernel(xn_ref, u_ref, vt_ref, s1_ref, s2_ref, w1_ref, w2_ref, tau_ref, x1_ref, g_ref,
                   o_ref, acc_ref, gate_ref, *, n_heads):
    e = pl.program_id(1)
    nkeys = s2_ref.shape[1]
    ec = u_ref.shape[0]
    groups = ec // nkeys

    @pl.when(e == 0)
    def _():
        acc_ref[...] = jnp.zeros_like(acc_ref)

    for gi in range(groups):
        i = e * groups + gi
        g = jnp.zeros((nkeys, s2_ref.shape[2]), F32)
        for h in range(n_heads):
            s = s1_ref[h, pl.ds(i, 1), :] + s2_ref[h]
            w = w1_ref[h, pl.ds(i, 1), :] * w2_ref[h]
            g = g + jnp.where(s >= tau_ref[h], w, 0.0)
        gate_ref[gi * nkeys:(gi + 1) * nkeys, :] = g

    a = lax.dot_general(u_ref[...], xn_ref[...], (((1,), (1,)), ((), ())), preferred_element_type=F32)
    act = (jax.nn.gelu(a) * gate_ref[...]).astype(BF16)
    acc_ref[...] += jnp.dot(vt_ref[...], act, preferred_element_type=F32)

    @pl.when(e == pl.num_programs(1) - 1)
    def _():
        x2 = x1_ref[...] + acc_ref[...].T
        o_ref[...] = _rms_norm(x2, g_ref[...])


def _pdense(xn, u, vt, s1, s2, w1, w2, tau, x1, g, tm, ec):
    t, d = xn.shape
    ne = u.shape[0]
    n_heads, nkeys, _ = s1.shape
    tok3 = pl.BlockSpec((n_heads, nkeys, tm), lambda i, e: (0, 0, i))
    return pl.pallas_call(
        functools.partial(_pdense_kernel, n_heads=n_heads),
        out_shape=jax.ShapeDtypeStruct((t, d), F32),
        grid=(t // tm, ne // ec),
        in_specs=[
            pl.BlockSpec((tm, d), lambda i, e: (i, 0)),
            pl.BlockSpec((ec, d), lambda i, e: (e, 0)),
            pl.BlockSpec((d, ec), lambda i, e: (0, e)),
            tok3, tok3, tok3, tok3,
            pl.BlockSpec((n_heads, 1, tm), lambda i, e: (0, 0, i)),
            pl.BlockSpec((tm, d), lambda i, e: (i, 0)),
            pl.BlockSpec((1, d), lambda i, e: (0, 0)),
        ],
        out_specs=pl.BlockSpec((tm, d), lambda i, e: (i, 0)),
        scratch_shapes=[pltpu.VMEM((d, tm), F32), pltpu.VMEM((ec, tm), F32)],
        compiler_params=_params("parallel", "arbitrary"),
        name="pdense",
    )(xn, u, vt, s1, s2, w1, w2, tau, x1, g)


def _layer(x, pos, an_g, w_in, gate_bias, conv_w, wa, wb, wo, fn_g, wq, keys, pu, pv, out_g, *,
           batch, seq, tiles):
    t, d = x.shape
    cw = wa.shape[0]
    aw = wb.shape[0]
    n_heads = aw // HEAD_DIM
    p_heads = keys.shape[0]

    inv_freq = 1.0 / (ROPE_THETA ** (jnp.arange(0, ROT_DIM, 2, dtype=F32) / ROT_DIM))
    invf = jnp.concatenate([inv_freq, inv_freq, jnp.zeros((HEAD_DIM - ROT_DIM,), F32)])[None, :]

    proj = _inproj(x, an_g[None, :], w_in.astype(BF16), tiles["inproj_tm"], tiles["inproj_tn"])
    q, k, v, kmean = _qkvprep(proj, pos, invf, 3 * cw, n_heads)
    kmean = kmean.reshape(batch, seq // MOBA_BLOCK, aw)
    attn = _attn(q, k, v, kmean, batch, seq, n_heads)
    x1, xn = _merge(proj, attn, x, conv_w, gate_bias[None, :], wa.astype(BF16), wb.astype(BF16),
                    wo.astype(BF16), fn_g[None, :], seq, tiles["merge_tm"])
    keys2 = keys.reshape(p_heads * 2, keys.shape[2], keys.shape[3])
    s1, s2, w1, w2, tau = _pscore(xn, wq.astype(BF16), keys2, p_heads, tiles["pscore_tm"])
    return _pdense(xn, pu.astype(BF16), pv.T.astype(BF16), s1, s2, w1, w2, tau, x1, out_g[None, :],
                   tiles["pdense_tm"], tiles["pdense_ec"])


def _tiles(t, seq):
    return dict(
        inproj_tm=min(512, t), inproj_tn=1024,
        merge_tm=min(256, seq),
        pscore_tm=min(256, t),
        pdense_tm=min(512, t), pdense_ec=512,
    )


def kernel(x, positions, attn_norm_g, w_in, gate_bias, conv_w, w_branch_conv, w_branch_attn, w_out,
           ffn_norm_g, w_peer_query, peer_sub_keys, peer_u, peer_v, final_norm_g):
    b, s, d = x.shape
    depth = w_in.shape[0]
    assert depth == 1, "final norm is fused into the last layer's kernel"
    t = b * s
    xt = x.reshape(t, d)
    pos = positions.reshape(t, 1)
    out = _layer(xt, pos, attn_norm_g[0], w_in[0], gate_bias[0], conv_w[0], w_branch_conv[0],
                 w_branch_attn[0], w_out[0], ffn_norm_g[0], w_peer_query[0], peer_sub_keys[0],
                 peer_u[0], peer_v[0], final_norm_g, batch=b, seq=s, tiles=_tiles(t, s))
    return out.reshape(b, s, d)
```

```python
import functools

import jax
import jax.numpy as jnp
from jax import lax
from jax.experimental import pallas as pl
from jax.experimental.pallas import tpu as pltpu

F32 = jnp.float32
BF16 = jnp.bfloat16

HEAD_DIM = 128
ROT_DIM = HEAD_DIM // 4
ROPE_THETA = 500000.0
MOBA_BLOCK = 256
MOBA_TOPK = 3
PEER_TOPK = 16
RMS_EPS = 1e-6

LANES = 128
SUBLANES = 8
VMEM_LIMIT_BYTES = 56 * 1024 * 1024

NEG_BIG = -1e30


def _params(*sem):
    return pltpu.CompilerParams(dimension_semantics=sem, vmem_limit_bytes=VMEM_LIMIT_BYTES)


def _resident(shape):
    zeros = (0,) * len(shape)
    return pl.BlockSpec(shape, lambda *_: zeros, pipeline_mode=pl.Buffered(1))


def _rms_norm(x, g):
    ms = jnp.mean(x * x, axis=-1, keepdims=True)
    return x * lax.rsqrt(ms + RMS_EPS) * g


def _inproj_kernel(x_ref, g_ref, w_ref, o_ref, h_ref):
    @pl.when(pl.program_id(1) == 0)
    def _():
        h_ref[...] = _rms_norm(x_ref[...], g_ref[...]).astype(BF16)

    o_ref[...] = jnp.dot(h_ref[...], w_ref[...], preferred_element_type=F32)


def _inproj(x, g, w, tm, tn):
    t, d = x.shape
    n = w.shape[1]
    return pl.pallas_call(
        _inproj_kernel,
        out_shape=jax.ShapeDtypeStruct((t, n), F32),
        grid=(t // tm, n // tn),
        in_specs=[
            pl.BlockSpec((tm, d), lambda i, j: (i, 0)),
            pl.BlockSpec((1, d), lambda i, j: (0, 0)),
            pl.BlockSpec((d, tn), lambda i, j: (0, j)),
        ],
        out_specs=pl.BlockSpec((tm, tn), lambda i, j: (i, j)),
        scratch_shapes=[pltpu.VMEM((tm, d), BF16)],
        compiler_params=_params("parallel", "arbitrary"),
        name="inproj",
    )(x, g, w)


def _qkvprep_kernel(q_ref, k_ref, v_ref, pos_ref, invf_ref, qo_ref, ko_ref, vo_ref, km_ref, *, n_heads):
    half = ROT_DIM // 2
    ang = pos_ref[...].astype(F32) * invf_ref[...]
    cos = jnp.cos(ang)
    sin = jnp.sin(ang)
    lane = lax.broadcasted_iota(jnp.int32, ang.shape, 1)
    sin_lo = jnp.where(lane < half, -sin, 0.0)
    sin_hi = jnp.where((lane >= half) & (lane < ROT_DIM), sin, 0.0)

    def rot(x):
        return x * cos + pltpu.roll(x, HEAD_DIM - half, 1) * sin_lo + pltpu.roll(x, half, 1) * sin_hi

    for h in range(n_heads):
        sl = slice(h * HEAD_DIM, (h + 1) * HEAD_DIM)
        qo_ref[:, sl] = rot(q_ref[:, sl])
        kr = rot(k_ref[:, sl])
        ko_ref[:, sl] = kr.astype(BF16)
        km_ref[0, :, sl] = jnp.mean(kr, axis=0, keepdims=True)
    vo_ref[...] = v_ref[...].astype(BF16)


def _qkvprep(proj, pos, invf, q_col, n_heads):
    t = proj.shape[0]
    w = n_heads * HEAD_DIM
    r = MOBA_BLOCK
    qb = q_col // w
    col = lambda c: pl.BlockSpec((r, w), lambda i: (i, c))
    return pl.pallas_call(
        functools.partial(_qkvprep_kernel, n_heads=n_heads),
        out_shape=(
            jax.ShapeDtypeStruct((t, w), F32),
            jax.ShapeDtypeStruct((t, w), BF16),
            jax.ShapeDtypeStruct((t, w), BF16),
            jax.ShapeDtypeStruct((t // r, 1, w), F32),
        ),
        grid=(t // r,),
        in_specs=[col(qb), col(qb + 1), col(qb + 2),
                  pl.BlockSpec((r, 1), lambda i: (i, 0)),
                  pl.BlockSpec((1, HEAD_DIM), lambda i: (0, 0))],
        out_specs=(
            pl.BlockSpec((r, w), lambda i: (i, 0)),
            pl.BlockSpec((r, w), lambda i: (i, 0)),
            pl.BlockSpec((r, w), lambda i: (i, 0)),
            pl.BlockSpec((1, 1, w), lambda i: (i, 0, 0)),
        ),
        compiler_params=_params("parallel"),
        name="qkvprep",
    )(proj, proj, proj, pos, invf)


_LOG2E = 1.4426950408889634


def _attn_kernel(q_ref, k_ref, v_ref, km_ref, o_ref, *, scale, heads):
    i = pl.program_id(2)
    blk = MOBA_BLOCK
    span = 2 * blk
    hd = HEAD_DIM
    nb = km_ref.shape[1]
    c = scale * _LOG2E
    nt = (((1,), (1,)), ((), ()))
    pair = lax.shift_right_logical(i, 1)
    odd = i - 2 * pair
    ones = jnp.ones((span, LANES), BF16)
    lane = lax.broadcasted_iota(jnp.int32, (span, LANES), 1)
    upper = jnp.where(lax.broadcasted_iota(jnp.int32, (span, LANES), 0) >= blk, 1, 0)
    row = lax.broadcasted_iota(jnp.int32, (blk, span), 0)
    colv = lax.broadcasted_iota(jnp.int32, (blk, span), 1)
    bidx = lax.broadcasted_iota(jnp.int32, (nb, blk), 0)

    def kv_pair(p, hs):
        start = pl.multiple_of(p * span, span)
        onehot = jnp.where(lane == 2 * p + upper, 1.0, 0.0).astype(BF16)
        k_aug = jnp.concatenate([k_ref[pl.ds(start, span), hs], onehot], axis=1)
        v_aug = jnp.concatenate([v_ref[pl.ds(start, span), hs], ones], axis=1)
        return k_aug, v_aug

    q_augs = []
    init = []
    for h in range(heads):
        hs = slice(h * hd, (h + 1) * hd)
        q = q_ref[:, hs]
        gate = lax.dot_general(km_ref[0, :, hs], q, nt,
                               precision=lax.Precision.HIGHEST, preferred_element_type=F32)
        gate = jnp.where(bidx < i, gate, -jnp.inf)
        sel = jnp.zeros(gate.shape, F32)
        for r in range(MOBA_TOPK):
            top = jnp.max(gate, axis=0, keepdims=True)
            first = jnp.min(jnp.where(gate == top, bidx, nb), axis=0, keepdims=True)
            hit = bidx == first
            sel = jnp.where(hit & (r < i), 1.0, sel)
            gate = jnp.where(hit, -jnp.inf, gate)
        bias_t = jnp.where((sel > 0.0) | (bidx == i), 0.0, NEG_BIG)
        bias_t = jnp.concatenate([bias_t, jnp.zeros((LANES - nb, blk), F32)], axis=0)
        bias = bias_t.T.astype(BF16)
        q_aug = jnp.concatenate([q.astype(BF16), bias], axis=1)
        q_augs.append(q_aug)

        k_aug, v_aug = kv_pair(pair, hs)
        s = lax.dot_general(q_aug, k_aug, nt, preferred_element_type=F32) * c
        s = jnp.where(colv - row > odd * blk, -jnp.inf, s)
        m0 = jnp.max(s, axis=-1, keepdims=True)
        p = jnp.exp2(s - m0).astype(BF16)
        init.append((m0, jnp.dot(p, v_aug, preferred_element_type=F32)))

    def body(j, carry):
        out = []
        for h in range(heads):
            hs = slice(h * hd, (h + 1) * hd)
            m, acc = carry[h]
            k_aug, v_aug = kv_pair(j, hs)
            s = lax.dot_general(q_augs[h], k_aug, nt, preferred_element_type=F32) * c
            m_new = jnp.maximum(m, jnp.max(s, axis=-1, keepdims=True))
            p = jnp.exp2(s - m_new).astype(BF16)
            acc = jnp.exp2(m - m_new) * acc + jnp.dot(p, v_aug, preferred_element_type=F32)
            out.append((m_new, acc))
        return tuple(out)

    final = lax.fori_loop(0, pair, body, tuple(init))
    for h in range(heads):
        acc = final[h][1]
        o_ref[:, h * hd:(h + 1) * hd] = (acc[:, :hd] / acc[:, hd:]).astype(o_ref.dtype)


def _attn(q, k, v, kmean, batch, seq, n_heads, heads):
    t, w = q.shape
    nq = seq // MOBA_BLOCK
    blk = MOBA_BLOCK
    hw = heads * HEAD_DIM
    assert nq % 2 == 0 and nq <= LANES, "key blocks are visited in pairs; mask columns fit one lane tile"
    return pl.pallas_call(
        functools.partial(_attn_kernel, scale=HEAD_DIM ** -0.5, heads=heads),
        out_shape=jax.ShapeDtypeStruct((t, w), BF16),
        grid=(batch, n_heads // heads, nq),
        in_specs=[
            pl.BlockSpec((blk, hw), lambda b, g, i: (b * nq + i, g)),
            pl.BlockSpec((seq, hw), lambda b, g, i: (b, g)),
            pl.BlockSpec((seq, hw), lambda b, g, i: (b, g)),
            pl.BlockSpec((1, nq, hw), lambda b, g, i: (b, 0, g)),
        ],
        out_specs=pl.BlockSpec((blk, hw), lambda b, g, i: (b * nq + i, g)),
        compiler_params=_params("parallel", "parallel", "arbitrary"),
        name="attn",
    )(q, k, v, kmean)


def _merge_kernel(cb_ref, cc_ref, cx_ref, pc_ref, px_ref, ga_ref, gb_ref, at_ref, x_ref,
                  cw_ref, bias_ref, wa_ref, wb_ref, wo_ref, ng_ref,
                  x1_ref, xn_ref, z_ref, *, seq):
    tm = cb_ref.shape[0]
    d = x_ref.shape[1]
    first = (pl.program_id(0) * tm) % seq == 0
    z = cc_ref[...] * cx_ref[...]
    z_prev = jnp.where(first, 0.0, pc_ref[...] * px_ref[...])
    z_ref[0:SUBLANES, :] = z_prev
    z_ref[SUBLANES:, :] = z
    cw = cw_ref[...]
    conv = (cw[2:3, :] * z
            + cw[1:2, :] * z_ref[SUBLANES - 1:SUBLANES - 1 + tm, :]
            + cw[0:1, :] * z_ref[SUBLANES - 2:SUBLANES - 2 + tm, :])
    u = (cb_ref[...] * conv).astype(BF16)
    y_conv = jnp.dot(u, wa_ref[...], preferred_element_type=F32)
    y_attn = jnp.dot(at_ref[...], wb_ref[...], preferred_element_type=F32)
    bias = bias_ref[...]
    merged = (jax.nn.sigmoid(ga_ref[...] + bias[:, :d]) * y_conv
              + jax.nn.sigmoid(gb_ref[...] + bias[:, d:]) * y_attn)
    x1 = x_ref[...] + jnp.dot(merged.astype(BF16), wo_ref[...], preferred_element_type=F32)
    x1_ref[...] = x1
    xn_ref[...] = _rms_norm(x1, ng_ref[...]).astype(BF16)


def _merge(proj, attn, x, conv_w, gate_bias, wa, wb, wo, ng, seq, tm):
    t, d = x.shape
    cw = wa.shape[0]
    aw = wb.shape[0]
    gcol = (3 * cw + 3 * aw) // d
    halo = lambda c: pl.BlockSpec(
        (SUBLANES, cw), lambda i: (jnp.maximum(i * (tm // SUBLANES) - 1, 0), c))
    return pl.pallas_call(
        functools.partial(_merge_kernel, seq=seq),
        out_shape=(jax.ShapeDtypeStruct((t, d), F32), jax.ShapeDtypeStruct((t, d), BF16)),
        grid=(t // tm,),
        in_specs=[
            pl.BlockSpec((tm, cw), lambda i: (i, 0)),
            pl.BlockSpec((tm, cw), lambda i: (i, 1)),
            pl.BlockSpec((tm, cw), lambda i: (i, 2)),
            halo(1), halo(2),
            pl.BlockSpec((tm, d), lambda i: (i, gcol)),
            pl.BlockSpec((tm, d), lambda i: (i, gcol + 1)),
            pl.BlockSpec((tm, aw), lambda i: (i, 0)),
            pl.BlockSpec((tm, d), lambda i: (i, 0)),
            _resident(conv_w.shape), _resident(gate_bias.shape),
            _resident(wa.shape), _resident(wb.shape), _resident(wo.shape), _resident(ng.shape),
        ],
        out_specs=(pl.BlockSpec((tm, d), lambda i: (i, 0)), pl.BlockSpec((tm, d), lambda i: (i, 0))),
        scratch_shapes=[pltpu.VMEM((tm + SUBLANES, cw), F32)],
        compiler_params=_params("parallel"),
        name="merge",
    )(proj, proj, proj, proj, proj, proj, proj, attn, x, conv_w, gate_bias, wa, wb, wo, ng)


def _sort_network(n):
    pairs = []

    def merge(lo, hi, r):
        step = r * 2
        if step < hi - lo:
            merge(lo, hi, step)
            merge(lo + r, hi, step)
            pairs.extend((i, i + r) for i in range(lo + r, hi - r, step))
        else:
            pairs.append((lo, lo + r))

    def sort(lo, hi):
        if hi - lo >= 1:
            mid = lo + (hi - lo) // 2
            sort(lo, mid)
            sort(mid + 1, hi)
            merge(lo, hi, 1)

    sort(0, n - 1)
    return pairs


def _compare_exchange(xs, i, j):
    a, b = xs[i], xs[j]
    if b is None:
        return
    if a is None:
        xs[i], xs[j] = b, None
        return
    xs[i], xs[j] = jnp.maximum(a, b), jnp.minimum(a, b)


def _sort_desc(xs):
    n = pl.next_power_of_2(len(xs))
    xs = list(xs) + [None] * (n - len(xs))
    for i, j in _sort_network(n):
        _compare_exchange(xs, i, j)
    return xs


def _top_per_column(s):
    n = s.shape[0] // SUBLANES
    xs = _sort_desc([s[SUBLANES * k:SUBLANES * (k + 1), :] for k in range(n)])
    r = SUBLANES // 2
    while r >= 1:
        xs = [jnp.maximum(xs[k], pltpu.roll(xs[n - 1 - k], SUBLANES - r, 0)) for k in range(n)]
        d = n // 2
        while d >= 1:
            for k in range(n):
                if k & d == 0:
                    _compare_exchange(xs, k, k + d)
            d //= 2
        r //= 2
    return xs


def _pscore_kernel(xn_ref, wq_ref, keys_ref, c1_ref, s2_ref, w1_ref, w2_ref, *, n_heads):
    nkeys, half = keys_ref.shape[1], keys_ref.shape[2]
    tm = xn_ref.shape[0]
    topk = PEER_TOPK
    qp = jnp.dot(xn_ref[...], wq_ref[...], preferred_element_type=F32)
    nt = (((1,), (1,)), ((), ()))
    sub = lax.broadcasted_iota(jnp.int32, (SUBLANES, tm), 0)

    tops = [[jnp.zeros((SUBLANES, tm), F32)] * (topk + 1) for _ in range(2)]
    for h in range(n_heads):
        for p, ref in enumerate((c1_ref, s2_ref)):
            c = (2 * h + p) * half
            s = lax.dot_general(keys_ref[2 * h + p], qp[:, c:c + half], nt,
                                precision=lax.Precision.HIGHEST, preferred_element_type=F32)
            ref[h] = s
            col = _top_per_column(s)
            below = jnp.max(jnp.where(s < col[topk - 1][0:1, :], s, -jnp.inf), axis=0, keepdims=True)
            col = [pltpu.roll(x, h, 0) if h else x for x in col] + [below]
            tops[p] = [jnp.where(sub == h, col[k], tops[p][k]) for k in range(topk + 1)]
    a, b = tops

    cand = [a[p] + b[q] for p in range(topk + 1) for q in range(topk + 1)
            if (p + 1) * (q + 1) <= topk + 1]
    best = _sort_desc(cand)[:topk + 1]
    z = best[0] * 0.0
    for k in range(topk):
        z = z + jnp.exp(best[k] - best[0])
    tau = 0.5 * (best[topk - 1] + best[topk])
    half_inv_z = 0.5 / z

    for h in range(n_heads):
        row = slice(h, h + 1)
        s1 = c1_ref[h]
        w1_ref[h] = jnp.exp(s1 - a[0][row, :]) * half_inv_z[row, :]
        w2_ref[h] = jnp.exp(s2_ref[h] - b[0][row, :])
        c1_ref[h] = tau[row, :] - s1


def _pscore(xn, wq, keys, n_heads, tm):
    t, d = xn.shape
    nkeys = keys.shape[1]
    assert n_heads == SUBLANES and nkeys == SUBLANES * PEER_TOPK
    big = jax.ShapeDtypeStruct((n_heads, nkeys, t), F32)
    bspec = pl.BlockSpec((n_heads, nkeys, tm), lambda i: (0, 0, i))
    return pl.pallas_call(
        functools.partial(_pscore_kernel, n_heads=n_heads),
        out_shape=(big, big, big, big),
        grid=(t // tm,),
        in_specs=[pl.BlockSpec((tm, d), lambda i: (i, 0)), _resident(wq.shape), _resident(keys.shape)],
        out_specs=(bspec, bspec, bspec, bspec),
        compiler_params=_params("parallel"),
        name="pscore",
    )(xn, wq, keys)


_GELU_C = 0.7978845608028654
_ACT_ROWS = 32


def _pdense_kernel(xn_ref, u_ref, vt_ref, c1_ref, s2_ref, w1_ref, w2_ref, x1_ref, g_ref,
                   o_ref, acc_ref, gate_ref, a_ref, act_ref, *, n_heads):
    e = pl.program_id(1)
    n_chunks = pl.num_programs(1) - 1
    nkeys = s2_ref.shape[1]
    ec = u_ref.shape[0]
    groups = ec // nkeys
    slot = e % 2

    @pl.when(e == 0)
    def _():
        acc_ref[...] = jnp.zeros_like(acc_ref)
        act_ref[1] = jnp.zeros(act_ref.shape[1:], act_ref.dtype)

    chunk = jnp.minimum(e, n_chunks - 1)
    tm = s2_ref.shape[2]
    d = vt_ref.shape[0]
    nt = (((1,), (1,)), ((), ()))

    def gate_group(gi):
        i = chunk * groups + gi
        w1_rows = [w1_ref[h, pl.ds(i, 1), :] for h in range(n_heads)]
        c1_rows = [c1_ref[h, pl.ds(i, 1), :] for h in range(n_heads)]
        for lt in range(tm // LANES):
            ln = slice(lt * LANES, (lt + 1) * LANES)
            g = jnp.zeros((nkeys, LANES), F32)
            for h in range(n_heads):
                w = w1_rows[h][:, ln] * w2_ref[h, :, ln]
                g = g + jnp.where(s2_ref[h, :, ln] >= c1_rows[h][:, ln], w, 0.0)
            gate_ref[gi * nkeys:(gi + 1) * nkeys, ln] = g

    def act_rows(r0, r1):
        for r in range(r0, r1, _ACT_ROWS):
            rows = slice(r, r + _ACT_ROWS)
            a = a_ref[rows, :]
            th = jnp.tanh(a * (_GELU_C + (_GELU_C * 0.044715) * (a * a)))
            act_ref[slot, rows, :] = ((a + a * th) * gate_ref[rows, :]).astype(BF16)

    halves = 2
    for m in range(halves):
        rows = slice(m * ec // halves, (m + 1) * ec // halves)
        a_ref[rows, :] = lax.dot_general(u_ref[rows, :], xn_ref[...], nt, preferred_element_type=F32)
        for gi in range(m * groups // halves, (m + 1) * groups // halves):
            gate_group(gi)
    for k in range(groups):
        rows = slice(k * d // groups, (k + 1) * d // groups)
        acc_ref[rows, :] += jnp.dot(vt_ref[rows, :], act_ref[1 - slot], preferred_element_type=F32)
        act_rows(k * nkeys, (k + 1) * nkeys)

    @pl.when(e == n_chunks)
    def _():
        x2 = x1_ref[...] + acc_ref[...].T
        o_ref[...] = _rms_norm(x2, g_ref[...])


def _pdense(xn, u, vt, c1, s2, w1, w2, x1, g, tm, ec):
    t, d = xn.shape
    n_chunks = u.shape[0] // ec
    n_heads, nkeys, _ = s2.shape
    tok3 = pl.BlockSpec((n_heads, nkeys, tm), lambda i, e: (0, 0, i))
    return pl.pallas_call(
        functools.partial(_pdense_kernel, n_heads=n_heads),
        out_shape=jax.ShapeDtypeStruct((t, d), F32),
        grid=(t // tm, n_chunks + 1),
        in_specs=[
            pl.BlockSpec((tm, d), lambda i, e: (i, 0)),
            pl.BlockSpec((ec, d), lambda i, e: (jnp.minimum(e, n_chunks - 1), 0)),
            pl.BlockSpec((d, ec), lambda i, e: (0, jnp.maximum(e - 1, 0))),
            tok3, tok3, tok3, tok3,
            pl.BlockSpec((tm, d), lambda i, e: (i, 0), pipeline_mode=pl.Buffered(1)),
            pl.BlockSpec((1, d), lambda i, e: (0, 0)),
        ],
        out_specs=pl.BlockSpec((tm, d), lambda i, e: (i, 0)),
        scratch_shapes=[pltpu.VMEM((d, tm), F32), pltpu.VMEM((ec, tm), F32),
                        pltpu.VMEM((ec, tm), F32), pltpu.VMEM((2, ec, tm), BF16)],
        compiler_params=_params("parallel", "arbitrary"),
        name="pdense",
    )(xn, u, vt, c1, s2, w1, w2, x1, g)


def _layer(x, pos, an_g, w_in, gate_bias, conv_w, wa, wb, wo, fn_g, wq, keys, pu, pv, out_g, *,
           batch, seq, tiles):
    t, d = x.shape
    cw = wa.shape[0]
    aw = wb.shape[0]
    n_heads = aw // HEAD_DIM
    p_heads = keys.shape[0]

    inv_freq = 1.0 / (ROPE_THETA ** (jnp.arange(0, ROT_DIM, 2, dtype=F32) / ROT_DIM))
    invf = jnp.concatenate([inv_freq, inv_freq, jnp.zeros((HEAD_DIM - ROT_DIM,), F32)])[None, :]

    proj = _inproj(x, an_g[None, :], w_in.astype(BF16), tiles["inproj_tm"], tiles["inproj_tn"])
    q, k, v, kmean = _qkvprep(proj, pos, invf, 3 * cw, n_heads)
    kmean = kmean.reshape(batch, seq // MOBA_BLOCK, aw)
    attn = _attn(q, k, v, kmean, batch, seq, n_heads, min(tiles["attn_heads"], n_heads))
    x1, xn = _merge(proj, attn, x, conv_w, gate_bias[None, :], wa.astype(BF16), wb.astype(BF16),
                    wo.astype(BF16), fn_g[None, :], seq, tiles["merge_tm"])
    keys2 = keys.reshape(p_heads * 2, keys.shape[2], keys.shape[3])
    c1, s2, w1, w2 = _pscore(xn, wq.astype(BF16), keys2, p_heads, tiles["pscore_tm"])
    return _pdense(xn, pu.astype(BF16), pv.T.astype(BF16), c1, s2, w1, w2, x1, out_g[None, :],
                   tiles["pdense_tm"], tiles["pdense_ec"])


def _tiles(t, seq):
    return dict(
        inproj_tm=min(512, t), inproj_tn=1024,
        attn_heads=4,
        merge_tm=min(256, seq),
        pscore_tm=min(512, t),
        pdense_tm=min(512, t), pdense_ec=512,
    )


def kernel(x, positions, attn_norm_g, w_in, gate_bias, conv_w, w_branch_conv, w_branch_attn, w_out,
           ffn_norm_g, w_peer_query, peer_sub_keys, peer_u, peer_v, final_norm_g):
    b, s, d = x.shape
    depth = w_in.shape[0]
    assert depth == 1, "final norm is fused into the last layer's kernel"
    t = b * s
    xt = x.reshape(t, d)
    pos = positions.reshape(t, 1)
    out = _layer(xt, pos, attn_norm_g[0], w_in[0], gate_bias[0], conv_w[0], w_branch_conv[0],
                 w_branch_attn[0], w_out[0], ffn_norm_g[0], w_peer_query[0], peer_sub_keys[0],
                 peer_u[0], peer_v[0], final_norm_g, batch=b, seq=s, tiles=_tiles(t, s))
    return out.reshape(b, s, d)
```

```python
import functools

import jax
import jax.numpy as jnp
from jax import lax
from jax.experimental import pallas as pl
from jax.experimental.pallas import tpu as pltpu

F32 = jnp.float32
BF16 = jnp.bfloat16

HEAD_DIM = 128
ROT_DIM = HEAD_DIM // 4
ROPE_THETA = 500000.0
MOBA_BLOCK = 256
MOBA_TOPK = 3
PEER_TOPK = 16
RMS_EPS = 1e-6

LANES = 128
SUBLANES = 8
VMEM_LIMIT_BYTES = 56 * 1024 * 1024

NEG_BIG = -1e30


def _params(*sem):
    return pltpu.CompilerParams(dimension_semantics=sem, vmem_limit_bytes=VMEM_LIMIT_BYTES)


def _resident(shape):
    zeros = (0,) * len(shape)
    return pl.BlockSpec(shape, lambda *_: zeros, pipeline_mode=pl.Buffered(1))


def _rms_norm(x, g):
    ms = jnp.mean(x * x, axis=-1, keepdims=True)
    return x * lax.rsqrt(ms + RMS_EPS) * g


def _inproj_kernel(x_ref, g_ref, w_ref, o_ref, h_ref):
    @pl.when(pl.program_id(1) == 0)
    def _():
        h_ref[...] = _rms_norm(x_ref[...], g_ref[...]).astype(BF16)

    o_ref[...] = jnp.dot(h_ref[...], w_ref[...], preferred_element_type=F32)


def _inproj(x, g, w, tm, tn):
    t, d = x.shape
    n = w.shape[1]
    return pl.pallas_call(
        _inproj_kernel,
        out_shape=jax.ShapeDtypeStruct((t, n), F32),
        grid=(t // tm, n // tn),
        in_specs=[
            pl.BlockSpec((tm, d), lambda i, j: (i, 0)),
            pl.BlockSpec((1, d), lambda i, j: (0, 0)),
            pl.BlockSpec((d, tn), lambda i, j: (0, j)),
        ],
        out_specs=pl.BlockSpec((tm, tn), lambda i, j: (i, j)),
        scratch_shapes=[pltpu.VMEM((tm, d), BF16)],
        compiler_params=_params("parallel", "arbitrary"),
        name="inproj",
    )(x, g, w)


def _qkvprep_kernel(q_ref, k_ref, v_ref, pos_ref, invf_ref, qo_ref, ko_ref, vo_ref, km_ref, *, n_heads):
    half = ROT_DIM // 2
    ang = pos_ref[...].astype(F32) * invf_ref[...]
    cos = jnp.cos(ang)
    sin = jnp.sin(ang)
    lane = lax.broadcasted_iota(jnp.int32, ang.shape, 1)
    sin_lo = jnp.where(lane < half, -sin, 0.0)
    sin_hi = jnp.where((lane >= half) & (lane < ROT_DIM), sin, 0.0)

    def rot(x):
        return x * cos + pltpu.roll(x, HEAD_DIM - half, 1) * sin_lo + pltpu.roll(x, half, 1) * sin_hi

    for h in range(n_heads):
        sl = slice(h * HEAD_DIM, (h + 1) * HEAD_DIM)
        qo_ref[:, sl] = rot(q_ref[:, sl])
        kr = rot(k_ref[:, sl])
        ko_ref[:, sl] = kr.astype(BF16)
        km_ref[0, :, sl] = jnp.mean(kr, axis=0, keepdims=True)
    vo_ref[...] = v_ref[...].astype(BF16)


def _qkvprep(proj, pos, invf, q_col, n_heads):
    t = proj.shape[0]
    w = n_heads * HEAD_DIM
    r = MOBA_BLOCK
    qb = q_col // w
    col = lambda c: pl.BlockSpec((r, w), lambda i: (i, c))
    return pl.pallas_call(
        functools.partial(_qkvprep_kernel, n_heads=n_heads),
        out_shape=(
            jax.ShapeDtypeStruct((t, w), F32),
            jax.ShapeDtypeStruct((t, w), BF16),
            jax.ShapeDtypeStruct((t, w), BF16),
            jax.ShapeDtypeStruct((t // r, 1, w), F32),
        ),
        grid=(t // r,),
        in_specs=[col(qb), col(qb + 1), col(qb + 2),
                  pl.BlockSpec((r, 1), lambda i: (i, 0)),
                  pl.BlockSpec((1, HEAD_DIM), lambda i: (0, 0))],
        out_specs=(
            pl.BlockSpec((r, w), lambda i: (i, 0)),
            pl.BlockSpec((r, w), lambda i: (i, 0)),
            pl.BlockSpec((r, w), lambda i: (i, 0)),
            pl.BlockSpec((1, 1, w), lambda i: (i, 0, 0)),
        ),
        compiler_params=_params("parallel"),
        name="qkvprep",
    )(proj, proj, proj, pos, invf)


_LOG2E = 1.4426950408889634


def _attn_kernel(q_ref, k_ref, v_ref, km_ref, o_ref, *, scale, heads):
    i = pl.program_id(2)
    blk = MOBA_BLOCK
    span = 2 * blk
    hd = HEAD_DIM
    nb = km_ref.shape[1]
    c = scale * _LOG2E
    nt = (((1,), (1,)), ((), ()))
    pair = lax.shift_right_logical(i, 1)
    odd = i - 2 * pair
    ones = jnp.ones((span, LANES), BF16)
    lane = lax.broadcasted_iota(jnp.int32, (span, LANES), 1)
    upper = jnp.where(lax.broadcasted_iota(jnp.int32, (span, LANES), 0) >= blk, 1, 0)
    row = lax.broadcasted_iota(jnp.int32, (blk, span), 0)
    colv = lax.broadcasted_iota(jnp.int32, (blk, span), 1)
    bidx = lax.broadcasted_iota(jnp.int32, (nb, blk), 0)

    def kv_pair(p, hs):
        start = pl.multiple_of(p * span, span)
        onehot = jnp.where(lane == 2 * p + upper, 1.0, 0.0).astype(BF16)
        k_aug = jnp.concatenate([k_ref[pl.ds(start, span), hs], onehot], axis=1)
        v_aug = jnp.concatenate([v_ref[pl.ds(start, span), hs], ones], axis=1)
        return k_aug, v_aug

    q_augs = []
    init = []
    for h in range(heads):
        hs = slice(h * hd, (h + 1) * hd)
        q = q_ref[:, hs]
        gate = lax.dot_general(km_ref[0, :, hs], q, nt,
                               precision=lax.Precision.HIGHEST, preferred_element_type=F32)
        gate = jnp.where(bidx < i, gate, -jnp.inf)
        sel = jnp.zeros(gate.shape, F32)
        for r in range(MOBA_TOPK):
            top = jnp.max(gate, axis=0, keepdims=True)
            first = jnp.min(jnp.where(gate == top, bidx, nb), axis=0, keepdims=True)
            hit = bidx == first
            sel = jnp.where(hit & (r < i), 1.0, sel)
            gate = jnp.where(hit, -jnp.inf, gate)
        bias_t = jnp.where((sel > 0.0) | (bidx == i), 0.0, NEG_BIG)
        bias_t = jnp.concatenate([bias_t, jnp.zeros((LANES - nb, blk), F32)], axis=0)
        bias = bias_t.T.astype(BF16)
        q_aug = jnp.concatenate([q.astype(BF16), bias], axis=1)
        q_augs.append(q_aug)

        k_aug, v_aug = kv_pair(pair, hs)
        s = lax.dot_general(q_aug, k_aug, nt, preferred_element_type=F32) * c
        s = jnp.where(colv - row > odd * blk, -jnp.inf, s)
        m0 = jnp.max(s, axis=-1, keepdims=True)
        p = jnp.exp2(s - m0).astype(BF16)
        init.append((m0, jnp.dot(p, v_aug, preferred_element_type=F32)))

    def body(j, carry):
        out = []
        for h in range(heads):
            hs = slice(h * hd, (h + 1) * hd)
            m, acc = carry[h]
            k_aug, v_aug = kv_pair(j, hs)
            s = lax.dot_general(q_augs[h], k_aug, nt, preferred_element_type=F32) * c
            m_new = jnp.maximum(m, jnp.max(s, axis=-1, keepdims=True))
            p = jnp.exp2(s - m_new).astype(BF16)
            acc = jnp.exp2(m - m_new) * acc + jnp.dot(p, v_aug, preferred_element_type=F32)
            out.append((m_new, acc))
        return tuple(out)

    final = lax.fori_loop(0, pair, body, tuple(init))
    for h in range(heads):
        acc = final[h][1]
        o_ref[:, h * hd:(h + 1) * hd] = (acc[:, :hd] / acc[:, hd:]).astype(o_ref.dtype)


def _attn(q, k, v, kmean, batch, seq, n_heads, heads):
    t, w = q.shape
    nq = seq // MOBA_BLOCK
    blk = MOBA_BLOCK
    hw = heads * HEAD_DIM
    assert nq % 2 == 0 and nq <= LANES, "key blocks are visited in pairs; mask columns fit one lane tile"
    return pl.pallas_call(
        functools.partial(_attn_kernel, scale=HEAD_DIM ** -0.5, heads=heads),
        out_shape=jax.ShapeDtypeStruct((t, w), BF16),
        grid=(batch, n_heads // heads, nq),
        in_specs=[
            pl.BlockSpec((blk, hw), lambda b, g, i: (b * nq + i, g)),
            pl.BlockSpec((seq, hw), lambda b, g, i: (b, g)),
            pl.BlockSpec((seq, hw), lambda b, g, i: (b, g)),
            pl.BlockSpec((1, nq, hw), lambda b, g, i: (b, 0, g)),
        ],
        out_specs=pl.BlockSpec((blk, hw), lambda b, g, i: (b * nq + i, g)),
        compiler_params=_params("parallel", "parallel", "arbitrary"),
        name="attn",
    )(q, k, v, kmean)


def _merge_kernel(cb_ref, cc_ref, cx_ref, pc_ref, px_ref, ga_ref, gb_ref, at_ref, x_ref,
                  cw_ref, bias_ref, wa_ref, wb_ref, wo_ref, ng_ref,
                  x1_ref, xn_ref, z_ref, *, seq):
    tm = cb_ref.shape[0]
    d = x_ref.shape[1]
    first = (pl.program_id(0) * tm) % seq == 0
    z = cc_ref[...] * cx_ref[...]
    z_prev = jnp.where(first, 0.0, pc_ref[...] * px_ref[...])
    z_ref[0:SUBLANES, :] = z_prev
    z_ref[SUBLANES:, :] = z
    cw = cw_ref[...]
    conv = (cw[2:3, :] * z
            + cw[1:2, :] * z_ref[SUBLANES - 1:SUBLANES - 1 + tm, :]
            + cw[0:1, :] * z_ref[SUBLANES - 2:SUBLANES - 2 + tm, :])
    u = (cb_ref[...] * conv).astype(BF16)
    y_conv = jnp.dot(u, wa_ref[...], preferred_element_type=F32)
    y_attn = jnp.dot(at_ref[...], wb_ref[...], preferred_element_type=F32)
    bias = bias_ref[...]
    merged = (jax.nn.sigmoid(ga_ref[...] + bias[:, :d]) * y_conv
              + jax.nn.sigmoid(gb_ref[...] + bias[:, d:]) * y_attn)
    x1 = x_ref[...] + jnp.dot(merged.astype(BF16), wo_ref[...], preferred_element_type=F32)
    x1_ref[...] = x1
    xn_ref[...] = _rms_norm(x1, ng_ref[...]).astype(BF16)


def _merge(proj, attn, x, conv_w, gate_bias, wa, wb, wo, ng, seq, tm):
    t, d = x.shape
    cw = wa.shape[0]
    aw = wb.shape[0]
    gcol = (3 * cw + 3 * aw) // d
    halo = lambda c: pl.BlockSpec(
        (SUBLANES, cw), lambda i: (jnp.maximum(i * (tm // SUBLANES) - 1, 0), c))
    return pl.pallas_call(
        functools.partial(_merge_kernel, seq=seq),
        out_shape=(jax.ShapeDtypeStruct((t, d), F32), jax.ShapeDtypeStruct((t, d), BF16)),
        grid=(t // tm,),
        in_specs=[
            pl.BlockSpec((tm, cw), lambda i: (i, 0)),
            pl.BlockSpec((tm, cw), lambda i: (i, 1)),
            pl.BlockSpec((tm, cw), lambda i: (i, 2)),
            halo(1), halo(2),
            pl.BlockSpec((tm, d), lambda i: (i, gcol)),
            pl.BlockSpec((tm, d), lambda i: (i, gcol + 1)),
            pl.BlockSpec((tm, aw), lambda i: (i, 0)),
            pl.BlockSpec((tm, d), lambda i: (i, 0)),
            _resident(conv_w.shape), _resident(gate_bias.shape),
            _resident(wa.shape), _resident(wb.shape), _resident(wo.shape), _resident(ng.shape),
        ],
        out_specs=(pl.BlockSpec((tm, d), lambda i: (i, 0)), pl.BlockSpec((tm, d), lambda i: (i, 0))),
        scratch_shapes=[pltpu.VMEM((tm + SUBLANES, cw), F32)],
        compiler_params=_params("parallel"),
        name="merge",
    )(proj, proj, proj, proj, proj, proj, proj, attn, x, conv_w, gate_bias, wa, wb, wo, ng)


def _sort_network(n):
    pairs = []

    def merge(lo, hi, r):
        step = r * 2
        if step < hi - lo:
            merge(lo, hi, step)
            merge(lo + r, hi, step)
            pairs.extend((i, i + r) for i in range(lo + r, hi - r, step))
        else:
            pairs.append((lo, lo + r))

    def sort(lo, hi):
        if hi - lo >= 1:
            mid = lo + (hi - lo) // 2
            sort(lo, mid)
            sort(mid + 1, hi)
            merge(lo, hi, 1)

    sort(0, n - 1)
    return pairs


def _compare_exchange(xs, i, j):
    a, b = xs[i], xs[j]
    if b is None:
        return
    if a is None:
        xs[i], xs[j] = b, None
        return
    xs[i], xs[j] = jnp.maximum(a, b), jnp.minimum(a, b)


def _sort_desc(xs):
    n = pl.next_power_of_2(len(xs))
    xs = list(xs) + [None] * (n - len(xs))
    for i, j in _sort_network(n):
        _compare_exchange(xs, i, j)
    return xs


def _top_per_column(s):
    n = s.shape[0] // SUBLANES
    xs = _sort_desc([s[SUBLANES * k:SUBLANES * (k + 1), :] for k in range(n)])
    r = SUBLANES // 2
    while r >= 1:
        xs = [jnp.maximum(xs[k], pltpu.roll(xs[n - 1 - k], SUBLANES - r, 0)) for k in range(n)]
        d = n // 2
        while d >= 1:
            for k in range(n):
                if k & d == 0:
                    _compare_exchange(xs, k, k + d)
            d //= 2
        r //= 2
    return xs


def _pscore_kernel(xn_ref, wq_ref, keys_ref, c1_ref, s2_ref, w1_ref, w2_ref, *, n_heads):
    nkeys, half = keys_ref.shape[1], keys_ref.shape[2]
    tm = xn_ref.shape[0]
    topk = PEER_TOPK
    qp = jnp.dot(xn_ref[...], wq_ref[...], preferred_element_type=F32)
    nt = (((1,), (1,)), ((), ()))
    sub = lax.broadcasted_iota(jnp.int32, (SUBLANES, tm), 0)

    tops = [[jnp.zeros((SUBLANES, tm), F32)] * (topk + 1) for _ in range(2)]
    for h in range(n_heads):
        for p, ref in enumerate((c1_ref, s2_ref)):
            c = (2 * h + p) * half
            s = lax.dot_general(keys_ref[2 * h + p], qp[:, c:c + half], nt,
                                precision=lax.Precision.HIGHEST, preferred_element_type=F32)
            ref[h] = s
            col = _top_per_column(s)
            below = jnp.max(jnp.where(s < col[topk - 1][0:1, :], s, -jnp.inf), axis=0, keepdims=True)
            col = [pltpu.roll(x, h, 0) if h else x for x in col] + [below]
            tops[p] = [jnp.where(sub == h, col[k], tops[p][k]) for k in range(topk + 1)]
    a, b = tops

    cand = [a[p] + b[q] for p in range(topk + 1) for q in range(topk + 1)
            if (p + 1) * (q + 1) <= topk + 1]
    best = _sort_desc(cand)[:topk + 1]
    z = best[0] * 0.0
    for k in range(topk):
        z = z + jnp.exp(best[k] - best[0])
    tau = 0.5 * (best[topk - 1] + best[topk])
    half_inv_z = 0.5 / z

    for h in range(n_heads):
        row = slice(h, h + 1)
        s1 = c1_ref[h]
        w1_ref[h] = jnp.exp(s1 - a[0][row, :]) * half_inv_z[row, :]
        w2_ref[h] = jnp.exp(s2_ref[h] - b[0][row, :])
        c1_ref[h] = tau[row, :] - s1


def _pscore(xn, wq, keys, n_heads, tm):
    t, d = xn.shape
    nkeys = keys.shape[1]
    assert n_heads == SUBLANES and nkeys == SUBLANES * PEER_TOPK
    big = jax.ShapeDtypeStruct((n_heads, nkeys, t), F32)
    bspec = pl.BlockSpec((n_heads, nkeys, tm), lambda i: (0, 0, i))
    return pl.pallas_call(
        functools.partial(_pscore_kernel, n_heads=n_heads),
        out_shape=(big, big, big, big),
        grid=(t // tm,),
        in_specs=[pl.BlockSpec((tm, d), lambda i: (i, 0)), _resident(wq.shape), _resident(keys.shape)],
        out_specs=(bspec, bspec, bspec, bspec),
        compiler_params=_params("parallel"),
        name="pscore",
    )(xn, wq, keys)


_GELU_C = 0.7978845608028654
_ACT_ROWS = 32


def _pdense_kernel(xn_ref, u_ref, vt_ref, c1_ref, s2_ref, w1_ref, w2_ref, x1_ref, g_ref,
                   o_ref, acc_ref, gate_ref, a_ref, act_ref, *, n_heads):
    e = pl.program_id(1)
    n_chunks = pl.num_programs(1) - 1
    nkeys = s2_ref.shape[1]
    ec = u_ref.shape[0]
    groups = ec // nkeys
    slot = e % 2

    @pl.when(e == 0)
    def _():
        acc_ref[...] = jnp.zeros_like(acc_ref)
        act_ref[1] = jnp.zeros(act_ref.shape[1:], act_ref.dtype)

    chunk = jnp.minimum(e, n_chunks - 1)
    tm = s2_ref.shape[2]
    d = vt_ref.shape[0]
    nt = (((1,), (1,)), ((), ()))

    def gate_group(gi):
        i = chunk * groups + gi
        w1_rows = [w1_ref[h, pl.ds(i, 1), :] for h in range(n_heads)]
        c1_rows = [c1_ref[h, pl.ds(i, 1), :] for h in range(n_heads)]
        for lt in range(tm // LANES):
            ln = slice(lt * LANES, (lt + 1) * LANES)
            g = jnp.zeros((nkeys, LANES), F32)
            for h in range(n_heads):
                w = w1_rows[h][:, ln] * w2_ref[h, :, ln]
                g = g + jnp.where(s2_ref[h, :, ln] >= c1_rows[h][:, ln], w, 0.0)
            gate_ref[gi * nkeys:(gi + 1) * nkeys, ln] = g

    def act_rows(r0, r1):
        for r in range(r0, r1, _ACT_ROWS):
            rows = slice(r, r + _ACT_ROWS)
            a = a_ref[rows, :]
            th = jnp.tanh(a * (_GELU_C + (_GELU_C * 0.044715) * (a * a)))
            act_ref[slot, rows, :] = ((a + a * th) * gate_ref[rows, :]).astype(BF16)

    halves = 2
    for m in range(halves):
        rows = slice(m * ec // halves, (m + 1) * ec // halves)
        a_ref[rows, :] = lax.dot_general(u_ref[rows, :], xn_ref[...], nt, preferred_element_type=F32)
        for gi in range(m * groups // halves, (m + 1) * groups // halves):
            gate_group(gi)
    for k in range(groups):
        rows = slice(k * d // groups, (k + 1) * d // groups)
        acc_ref[rows, :] += jnp.dot(vt_ref[rows, :], act_ref[1 - slot], preferred_element_type=F32)
        act_rows(k * nkeys, (k + 1) * nkeys)

    @pl.when(e == n_chunks)
    def _():
        x2 = x1_ref[...] + acc_ref[...].T
        o_ref[...] = _rms_norm(x2, g_ref[...])


def _pdense(xn, u, vt, c1, s2, w1, w2, x1, g, tm, ec):
    t, d = xn.shape
    n_chunks = u.shape[0] // ec
    n_heads, nkeys, _ = s2.shape
    tok3 = pl.BlockSpec((n_heads, nkeys, tm), lambda i, e: (0, 0, i))
    return pl.pallas_call(
        functools.partial(_pdense_kernel, n_heads=n_heads),
        out_shape=jax.ShapeDtypeStruct((t, d), F32),
        grid=(t // tm, n_chunks + 1),
        in_specs=[
            pl.BlockSpec((tm, d), lambda i, e: (i, 0)),
            pl.BlockSpec((ec, d), lambda i, e: (jnp.minimum(e, n_chunks - 1), 0)),
            pl.BlockSpec((d, ec), lambda i, e: (0, jnp.maximum(e - 1, 0))),
            tok3, tok3, tok3, tok3,
            pl.BlockSpec((tm, d), lambda i, e: (i, 0), pipeline_mode=pl.Buffered(1)),
            pl.BlockSpec((1, d), lambda i, e: (0, 0)),
        ],
        out_specs=pl.BlockSpec((tm, d), lambda i, e: (i, 0)),
        scratch_shapes=[pltpu.VMEM((d, tm), F32), pltpu.VMEM((ec, tm), F32),
                        pltpu.VMEM((ec, tm), F32), pltpu.VMEM((2, ec, tm), BF16)],
        compiler_params=_params("parallel", "arbitrary"),
        name="pdense",
    )(xn, u, vt, c1, s2, w1, w2, x1, g)


def _layer(x, pos, an_g, w_in, gate_bias, conv_w, wa, wb, wo, fn_g, wq, keys, pu, pv, out_g, *,
           batch, seq, tiles):
    t, d = x.shape
    cw = wa.shape[0]
    aw = wb.shape[0]
    n_heads = aw // HEAD_DIM
    p_heads = keys.shape[0]

    inv_freq = 1.0 / (ROPE_THETA ** (jnp.arange(0, ROT_DIM, 2, dtype=F32) / ROT_DIM))
    invf = jnp.concatenate([inv_freq, inv_freq, jnp.zeros((HEAD_DIM - ROT_DIM,), F32)])[None, :]

    proj = _inproj(x, an_g[None, :], w_in.astype(BF16), tiles["inproj_tm"], tiles["inproj_tn"])
    q, k, v, kmean = _qkvprep(proj, pos, invf, 3 * cw, n_heads)
    kmean = kmean.reshape(batch, seq // MOBA_BLOCK, aw)
    attn = _attn(q, k, v, kmean, batch, seq, n_heads, min(tiles["attn_heads"], n_heads))
    x1, xn = _merge(proj, attn, x, conv_w, gate_bias[None, :], wa.astype(BF16), wb.astype(BF16),
                    wo.astype(BF16), fn_g[None, :], seq, tiles["merge_tm"])
    keys2 = keys.reshape(p_heads * 2, keys.shape[2], keys.shape[3])
    c1, s2, w1, w2 = _pscore(xn, wq.astype(BF16), keys2, p_heads, tiles["pscore_tm"])
    return _pdense(xn, pu.astype(BF16), pv.T.astype(BF16), c1, s2, w1, w2, x1, out_g[None, :],
                   tiles["pdense_tm"], tiles["pdense_ec"])


def _tiles(t, seq):
    return dict(
        inproj_tm=min(1024, t), inproj_tn=1024,
        attn_heads=8,
        merge_tm=min(256, seq),
        pscore_tm=min(512, t),
        pdense_tm=min(512, t), pdense_ec=512,
    )


def kernel(x, positions, attn_norm_g, w_in, gate_bias, conv_w, w_branch_conv, w_branch_attn, w_out,
           ffn_norm_g, w_peer_query, peer_sub_keys, peer_u, peer_v, final_norm_g):
    b, s, d = x.shape
    depth = w_in.shape[0]
    assert depth == 1, "final norm is fused into the last layer's kernel"
    t = b * s
    xt = x.reshape(t, d)
    pos = positions.reshape(t, 1)
    out = _layer(xt, pos, attn_norm_g[0], w_in[0], gate_bias[0], conv_w[0], w_branch_conv[0],
                 w_branch_attn[0], w_out[0], ffn_norm_g[0], w_peer_query[0], peer_sub_keys[0],
                 peer_u[0], peer_v[0], final_norm_g, batch=b, seq=s, tiles=_tiles(t, s))
    return out.reshape(b, s, d)
```

```python
import functools

import jax
import jax.numpy as jnp
from jax import lax
from jax.experimental import pallas as pl
from jax.experimental.pallas import tpu as pltpu

F32 = jnp.float32
BF16 = jnp.bfloat16

HEAD_DIM = 128
ROT_DIM = HEAD_DIM // 4
ROPE_THETA = 500000.0
MOBA_BLOCK = 256
MOBA_TOPK = 3
PEER_TOPK = 16
RMS_EPS = 1e-6

LANES = 128
SUBLANES = 8
VMEM_LIMIT_BYTES = 56 * 1024 * 1024

NEG_BIG = -1e30


def _params(*sem):
    return pltpu.CompilerParams(dimension_semantics=sem, vmem_limit_bytes=VMEM_LIMIT_BYTES)


def _resident(shape):
    zeros = (0,) * len(shape)
    return pl.BlockSpec(shape, lambda *_: zeros, pipeline_mode=pl.Buffered(1))


def _dot_nt_3pass(a, b):
    nt = (((1,), (1,)), ((), ()))
    a_hi, b_hi = a.astype(BF16), b.astype(BF16)
    a_lo = (a - a_hi.astype(F32)).astype(BF16)
    b_lo = (b - b_hi.astype(F32)).astype(BF16)
    dot = lambda x, y: lax.dot_general(x, y, nt, preferred_element_type=F32)
    return dot(a_hi, b_hi) + (dot(a_hi, b_lo) + dot(a_lo, b_hi))


def _rms_norm(x, g):
    ms = jnp.mean(x * x, axis=-1, keepdims=True)
    return x * lax.rsqrt(ms + RMS_EPS) * g


def _inproj_kernel(x_ref, g_ref, w_ref, o_ref, h_ref):
    @pl.when(pl.program_id(1) == 0)
    def _():
        h_ref[...] = _rms_norm(x_ref[...], g_ref[...]).astype(BF16)

    o_ref[...] = jnp.dot(h_ref[...], w_ref[...], preferred_element_type=F32)


def _inproj(x, g, w, tm, tn):
    t, d = x.shape
    n = w.shape[1]
    return pl.pallas_call(
        _inproj_kernel,
        out_shape=jax.ShapeDtypeStruct((t, n), F32),
        grid=(t // tm, n // tn),
        in_specs=[
            pl.BlockSpec((tm, d), lambda i, j: (i, 0)),
            pl.BlockSpec((1, d), lambda i, j: (0, 0)),
            pl.BlockSpec((d, tn), lambda i, j: (0, j)),
        ],
        out_specs=pl.BlockSpec((tm, tn), lambda i, j: (i, j)),
        scratch_shapes=[pltpu.VMEM((tm, d), BF16)],
        compiler_params=_params("parallel", "arbitrary"),
        name="inproj",
    )(x, g, w)


def _qkvprep_kernel(q_ref, k_ref, v_ref, pos_ref, invf_ref, qo_ref, ko_ref, vo_ref, km_ref, *, n_heads):
    half = ROT_DIM // 2
    ang = pos_ref[...].astype(F32) * invf_ref[...]
    cos = jnp.cos(ang)
    sin = jnp.sin(ang)
    lane = lax.broadcasted_iota(jnp.int32, ang.shape, 1)
    sin_lo = jnp.where(lane < half, -sin, 0.0)
    sin_hi = jnp.where((lane >= half) & (lane < ROT_DIM), sin, 0.0)

    def rot(x):
        return x * cos + pltpu.roll(x, HEAD_DIM - half, 1) * sin_lo + pltpu.roll(x, half, 1) * sin_hi

    for h in range(n_heads):
        sl = slice(h * HEAD_DIM, (h + 1) * HEAD_DIM)
        qo_ref[:, sl] = rot(q_ref[:, sl])
        kr = rot(k_ref[:, sl])
        ko_ref[:, sl] = kr.astype(BF16)
        km_ref[0, :, sl] = jnp.mean(kr, axis=0, keepdims=True)
    vo_ref[...] = v_ref[...].astype(BF16)


def _qkvprep(proj, pos, invf, q_col, n_heads):
    t = proj.shape[0]
    w = n_heads * HEAD_DIM
    r = MOBA_BLOCK
    qb = q_col // w
    col = lambda c: pl.BlockSpec((r, w), lambda i: (i, c))
    return pl.pallas_call(
        functools.partial(_qkvprep_kernel, n_heads=n_heads),
        out_shape=(
            jax.ShapeDtypeStruct((t, w), F32),
            jax.ShapeDtypeStruct((t, w), BF16),
            jax.ShapeDtypeStruct((t, w), BF16),
            jax.ShapeDtypeStruct((t // r, 1, w), F32),
        ),
        grid=(t // r,),
        in_specs=[col(qb), col(qb + 1), col(qb + 2),
                  pl.BlockSpec((r, 1), lambda i: (i, 0)),
                  pl.BlockSpec((1, HEAD_DIM), lambda i: (0, 0))],
        out_specs=(
            pl.BlockSpec((r, w), lambda i: (i, 0)),
            pl.BlockSpec((r, w), lambda i: (i, 0)),
            pl.BlockSpec((r, w), lambda i: (i, 0)),
            pl.BlockSpec((1, 1, w), lambda i: (i, 0, 0)),
        ),
        compiler_params=_params("parallel"),
        name="qkvprep",
    )(proj, proj, proj, pos, invf)


_LOG2E = 1.4426950408889634


def _attn_kernel(q_ref, k_ref, v_ref, km_ref, o_ref, *, scale, heads):
    i = pl.program_id(2)
    blk = MOBA_BLOCK
    span = 2 * blk
    hd = HEAD_DIM
    nb = km_ref.shape[1]
    c = scale * _LOG2E
    nt = (((1,), (1,)), ((), ()))
    pair = lax.shift_right_logical(i, 1)
    odd = i - 2 * pair
    ones = jnp.ones((span, LANES), BF16)
    lane = lax.broadcasted_iota(jnp.int32, (span, LANES), 1)
    upper = jnp.where(lax.broadcasted_iota(jnp.int32, (span, LANES), 0) >= blk, 1, 0)
    row = lax.broadcasted_iota(jnp.int32, (blk, span), 0)
    colv = lax.broadcasted_iota(jnp.int32, (blk, span), 1)
    bidx = lax.broadcasted_iota(jnp.int32, (nb, blk), 0)

    def kv_pair(p, hs):
        start = pl.multiple_of(p * span, span)
        onehot = jnp.where(lane == 2 * p + upper, 1.0, 0.0).astype(BF16)
        k_aug = jnp.concatenate([k_ref[pl.ds(start, span), hs], onehot], axis=1)
        v_aug = jnp.concatenate([v_ref[pl.ds(start, span), hs], ones], axis=1)
        return k_aug, v_aug

    q_augs = []
    init = []
    for h in range(heads):
        hs = slice(h * hd, (h + 1) * hd)
        q = q_ref[:, hs]
        gate = _dot_nt_3pass(km_ref[0, :, hs], q)
        gate = jnp.where(bidx < i, gate, -jnp.inf)
        sel = jnp.zeros(gate.shape, F32)
        for r in range(MOBA_TOPK):
            top = jnp.max(gate, axis=0, keepdims=True)
            first = jnp.min(jnp.where(gate == top, bidx, nb), axis=0, keepdims=True)
            hit = bidx == first
            sel = jnp.where(hit & (r < i), 1.0, sel)
            gate = jnp.where(hit, -jnp.inf, gate)
        bias_t = jnp.where((sel > 0.0) | (bidx == i), 0.0, NEG_BIG)
        bias_t = jnp.concatenate([bias_t, jnp.zeros((LANES - nb, blk), F32)], axis=0)
        bias = bias_t.T.astype(BF16)
        q_aug = jnp.concatenate([q.astype(BF16), bias], axis=1)
        q_augs.append(q_aug)

        k_aug, v_aug = kv_pair(pair, hs)
        s = lax.dot_general(q_aug, k_aug, nt, preferred_element_type=F32) * c
        s = jnp.where(colv - row > odd * blk, -jnp.inf, s)
        m0 = jnp.max(s, axis=-1, keepdims=True)
        p = jnp.exp2(s - m0).astype(BF16)
        init.append((m0, jnp.dot(p, v_aug, preferred_element_type=F32)))

    def body(j, carry):
        out = []
        for h in range(heads):
            hs = slice(h * hd, (h + 1) * hd)
            m, acc = carry[h]
            k_aug, v_aug = kv_pair(j, hs)
            s = lax.dot_general(q_augs[h], k_aug, nt, preferred_element_type=F32) * c
            m_new = jnp.maximum(m, jnp.max(s, axis=-1, keepdims=True))
            p = jnp.exp2(s - m_new).astype(BF16)
            acc = jnp.exp2(m - m_new) * acc + jnp.dot(p, v_aug, preferred_element_type=F32)
            out.append((m_new, acc))
        return tuple(out)

    final = lax.fori_loop(0, pair, body, tuple(init))
    for h in range(heads):
        acc = final[h][1]
        o_ref[:, h * hd:(h + 1) * hd] = (acc[:, :hd] / acc[:, hd:]).astype(o_ref.dtype)


def _attn(q, k, v, kmean, batch, seq, n_heads, heads):
    t, w = q.shape
    nq = seq // MOBA_BLOCK
    blk = MOBA_BLOCK
    hw = heads * HEAD_DIM
    assert nq % 2 == 0 and nq <= LANES, "key blocks are visited in pairs; mask columns fit one lane tile"
    return pl.pallas_call(
        functools.partial(_attn_kernel, scale=HEAD_DIM ** -0.5, heads=heads),
        out_shape=jax.ShapeDtypeStruct((t, w), BF16),
        grid=(batch, n_heads // heads, nq),
        in_specs=[
            pl.BlockSpec((blk, hw), lambda b, g, i: (b * nq + i, g)),
            pl.BlockSpec((seq, hw), lambda b, g, i: (b, g)),
            pl.BlockSpec((seq, hw), lambda b, g, i: (b, g)),
            pl.BlockSpec((1, nq, hw), lambda b, g, i: (b, 0, g)),
        ],
        out_specs=pl.BlockSpec((blk, hw), lambda b, g, i: (b * nq + i, g)),
        compiler_params=_params("parallel", "parallel", "arbitrary"),
        name="attn",
    )(q, k, v, kmean)


def _merge_kernel(cb_ref, cc_ref, cx_ref, pc_ref, px_ref, ga_ref, gb_ref, at_ref, x_ref,
                  cw_ref, bias_ref, wa_ref, wb_ref, wo_ref, ng_ref,
                  x1_ref, xn_ref, z_ref, *, seq):
    tm = cb_ref.shape[0]
    d = x_ref.shape[1]
    first = (pl.program_id(0) * tm) % seq == 0
    z = cc_ref[...] * cx_ref[...]
    z_prev = jnp.where(first, 0.0, pc_ref[...] * px_ref[...])
    z_ref[0:SUBLANES, :] = z_prev
    z_ref[SUBLANES:, :] = z
    cw = cw_ref[...]
    conv = (cw[2:3, :] * z
            + cw[1:2, :] * z_ref[SUBLANES - 1:SUBLANES - 1 + tm, :]
            + cw[0:1, :] * z_ref[SUBLANES - 2:SUBLANES - 2 + tm, :])
    u = (cb_ref[...] * conv).astype(BF16)
    y_conv = jnp.dot(u, wa_ref[...], preferred_element_type=F32)
    y_attn = jnp.dot(at_ref[...], wb_ref[...], preferred_element_type=F32)
    bias = bias_ref[...]
    merged = (jax.nn.sigmoid(ga_ref[...] + bias[:, :d]) * y_conv
              + jax.nn.sigmoid(gb_ref[...] + bias[:, d:]) * y_attn)
    x1 = x_ref[...] + jnp.dot(merged.astype(BF16), wo_ref[...], preferred_element_type=F32)
    x1_ref[...] = x1
    xn_ref[...] = _rms_norm(x1, ng_ref[...]).astype(BF16)


def _merge(proj, attn, x, conv_w, gate_bias, wa, wb, wo, ng, seq, tm):
    t, d = x.shape
    cw = wa.shape[0]
    aw = wb.shape[0]
    gcol = (3 * cw + 3 * aw) // d
    halo = lambda c: pl.BlockSpec(
        (SUBLANES, cw), lambda i: (jnp.maximum(i * (tm // SUBLANES) - 1, 0), c))
    return pl.pallas_call(
        functools.partial(_merge_kernel, seq=seq),
        out_shape=(jax.ShapeDtypeStruct((t, d), F32), jax.ShapeDtypeStruct((t, d), BF16)),
        grid=(t // tm,),
        in_specs=[
            pl.BlockSpec((tm, cw), lambda i: (i, 0)),
            pl.BlockSpec((tm, cw), lambda i: (i, 1)),
            pl.BlockSpec((tm, cw), lambda i: (i, 2)),
            halo(1), halo(2),
            pl.BlockSpec((tm, d), lambda i: (i, gcol)),
            pl.BlockSpec((tm, d), lambda i: (i, gcol + 1)),
            pl.BlockSpec((tm, aw), lambda i: (i, 0)),
            pl.BlockSpec((tm, d), lambda i: (i, 0)),
            _resident(conv_w.shape), _resident(gate_bias.shape),
            _resident(wa.shape), _resident(wb.shape), _resident(wo.shape), _resident(ng.shape),
        ],
        out_specs=(pl.BlockSpec((tm, d), lambda i: (i, 0)), pl.BlockSpec((tm, d), lambda i: (i, 0))),
        scratch_shapes=[pltpu.VMEM((tm + SUBLANES, cw), F32)],
        compiler_params=_params("parallel"),
        name="merge",
    )(proj, proj, proj, proj, proj, proj, proj, attn, x, conv_w, gate_bias, wa, wb, wo, ng)


def _sort_network(n):
    pairs = []

    def merge(lo, hi, r):
        step = r * 2
        if step < hi - lo:
            merge(lo, hi, step)
            merge(lo + r, hi, step)
            pairs.extend((i, i + r) for i in range(lo + r, hi - r, step))
        else:
            pairs.append((lo, lo + r))

    def sort(lo, hi):
        if hi - lo >= 1:
            mid = lo + (hi - lo) // 2
            sort(lo, mid)
            sort(mid + 1, hi)
            merge(lo, hi, 1)

    sort(0, n - 1)
    return pairs


def _compare_exchange(xs, i, j):
    a, b = xs[i], xs[j]
    if b is None:
        return
    if a is None:
        xs[i], xs[j] = b, None
        return
    xs[i], xs[j] = jnp.maximum(a, b), jnp.minimum(a, b)


def _sort_desc(xs):
    n = pl.next_power_of_2(len(xs))
    xs = list(xs) + [None] * (n - len(xs))
    for i, j in _sort_network(n):
        _compare_exchange(xs, i, j)
    return xs


def _top_per_column(s):
    n = s.shape[0] // SUBLANES
    xs = _sort_desc([s[SUBLANES * k:SUBLANES * (k + 1), :] for k in range(n)])
    r = SUBLANES // 2
    while r >= 1:
        xs = [jnp.maximum(xs[k], pltpu.roll(xs[n - 1 - k], SUBLANES - r, 0)) for k in range(n)]
        d = n // 2
        while d >= 1:
            for k in range(n):
                if k & d == 0:
                    _compare_exchange(xs, k, k + d)
            d //= 2
        r //= 2
    return xs


def _pscore_kernel(xn_ref, wq_ref, keys_ref, c1_ref, s2_ref, w1_ref, w2_ref, *, n_heads):
    nkeys, half = keys_ref.shape[1], keys_ref.shape[2]
    tm = xn_ref.shape[0]
    topk = PEER_TOPK
    qp = jnp.dot(xn_ref[...], wq_ref[...], preferred_element_type=F32)
    nt = (((1,), (1,)), ((), ()))
    sub = lax.broadcasted_iota(jnp.int32, (SUBLANES, tm), 0)

    tops = [[jnp.zeros((SUBLANES, tm), F32)] * (topk + 1) for _ in range(2)]
    for h in range(n_heads):
        for p, ref in enumerate((c1_ref, s2_ref)):
            c = (2 * h + p) * half
            s = _dot_nt_3pass(keys_ref[2 * h + p], qp[:, c:c + half])
            ref[h] = s
            col = _top_per_column(s)
            below = jnp.max(jnp.where(s < col[topk - 1][0:1, :], s, -jnp.inf), axis=0, keepdims=True)
            col = [pltpu.roll(x, h, 0) if h else x for x in col] + [below]
            tops[p] = [jnp.where(sub == h, col[k], tops[p][k]) for k in range(topk + 1)]
    a, b = tops

    cand = [a[p] + b[q] for p in range(topk + 1) for q in range(topk + 1)
            if (p + 1) * (q + 1) <= topk + 1]
    best = _sort_desc(cand)[:topk + 1]
    z = best[0] * 0.0
    for k in range(topk):
        z = z + jnp.exp(best[k] - best[0])
    tau = 0.5 * (best[topk - 1] + best[topk])
    half_inv_z = 0.5 / z

    for h in range(n_heads):
        row = slice(h, h + 1)
        s1 = c1_ref[h]
        w1_ref[h] = jnp.exp(s1 - a[0][row, :]) * half_inv_z[row, :]
        w2_ref[h] = jnp.exp(s2_ref[h] - b[0][row, :])
        c1_ref[h] = tau[row, :] - s1


def _pscore(xn, wq, keys, n_heads, tm):
    t, d = xn.shape
    nkeys = keys.shape[1]
    assert n_heads == SUBLANES and nkeys == SUBLANES * PEER_TOPK
    big = jax.ShapeDtypeStruct((n_heads, nkeys, t), F32)
    bspec = pl.BlockSpec((n_heads, nkeys, tm), lambda i: (0, 0, i))
    return pl.pallas_call(
        functools.partial(_pscore_kernel, n_heads=n_heads),
        out_shape=(big, big, big, big),
        grid=(t // tm,),
        in_specs=[pl.BlockSpec((tm, d), lambda i: (i, 0)), _resident(wq.shape), _resident(keys.shape)],
        out_specs=(bspec, bspec, bspec, bspec),
        compiler_params=_params("parallel"),
        name="pscore",
    )(xn, wq, keys)


_GELU_C = 0.7978845608028654
_ACT_ROWS = 32


def _pdense_kernel(xn_ref, u_ref, vt_ref, c1_ref, s2_ref, w1_ref, w2_ref, x1_ref, g_ref,
                   o_ref, acc_ref, gate_ref, a_ref, act_ref, *, n_heads):
    e = pl.program_id(1)
    n_chunks = pl.num_programs(1) - 1
    nkeys = s2_ref.shape[1]
    ec = u_ref.shape[0]
    tm = s2_ref.shape[2]
    nt = (((1,), (1,)), ((), ()))

    @pl.when(e == 0)
    def _():
        acc_ref[...] = jnp.zeros_like(acc_ref)
        act_ref[1] = jnp.zeros(act_ref.shape[1:], act_ref.dtype)

    groups = ec // nkeys
    slot = e % 2
    chunk = jnp.minimum(e, n_chunks - 1)
    d = vt_ref.shape[0]

    def gate_group(gi):
        i = chunk * groups + gi
        w1_rows = [w1_ref[h, pl.ds(i, 1), :] for h in range(n_heads)]
        c1_rows = [c1_ref[h, pl.ds(i, 1), :] for h in range(n_heads)]
        for lt in range(tm // LANES):
            ln = slice(lt * LANES, (lt + 1) * LANES)
            g = jnp.zeros((nkeys, LANES), F32)
            for h in range(n_heads):
                w = w1_rows[h][:, ln] * w2_ref[h, :, ln]
                g = g + jnp.where(s2_ref[h, :, ln] >= c1_rows[h][:, ln], w, 0.0)
            gate_ref[gi * nkeys:(gi + 1) * nkeys, ln] = g

    def act_rows(r0, r1):
        for r in range(r0, r1, _ACT_ROWS):
            rows = slice(r, r + _ACT_ROWS)
            a = a_ref[rows, :]
            th = jnp.tanh(a * (_GELU_C + (_GELU_C * 0.044715) * (a * a)))
            act_ref[slot, rows, :] = ((a + a * th) * gate_ref[rows, :]).astype(BF16)

    halves = 2
    for m in range(halves):
        rows = slice(m * ec // halves, (m + 1) * ec // halves)
        a_ref[rows, :] = lax.dot_general(u_ref[rows, :], xn_ref[...], nt, preferred_element_type=F32)
        for gi in range(m * groups // halves, (m + 1) * groups // halves):
            gate_group(gi)
    for k in range(groups):
        rows = slice(k * d // groups, (k + 1) * d // groups)
        acc_ref[rows, :] += jnp.dot(vt_ref[rows, :], act_ref[1 - slot], preferred_element_type=F32)
        act_rows(k * nkeys, (k + 1) * nkeys)

    @pl.when(e == n_chunks)
    def _():
        x2 = x1_ref[...] + acc_ref[...].T
        o_ref[...] = _rms_norm(x2, g_ref[...])


def _pdense(xn, u, vt, c1, s2, w1, w2, x1, g, tm, ec):
    t, d = xn.shape
    n_chunks = u.shape[0] // ec
    n_heads, nkeys, _ = s2.shape
    tok3 = pl.BlockSpec((n_heads, nkeys, tm), lambda i, e: (0, 0, i))
    return pl.pallas_call(
        functools.partial(_pdense_kernel, n_heads=n_heads),
        out_shape=jax.ShapeDtypeStruct((t, d), F32),
        grid=(t // tm, n_chunks + 1),
        in_specs=[
            pl.BlockSpec((tm, d), lambda i, e: (i, 0)),
            pl.BlockSpec((ec, d), lambda i, e: (jnp.minimum(e, n_chunks - 1), 0)),
            pl.BlockSpec((d, ec), lambda i, e: (0, jnp.maximum(e - 1, 0))),
            tok3, tok3, tok3, tok3,
            pl.BlockSpec((tm, d), lambda i, e: (i, 0), pipeline_mode=pl.Buffered(1)),
            pl.BlockSpec((1, d), lambda i, e: (0, 0)),
        ],
        out_specs=pl.BlockSpec((tm, d), lambda i, e: (i, 0)),
        scratch_shapes=[pltpu.VMEM((d, tm), F32), pltpu.VMEM((ec, tm), F32),
                        pltpu.VMEM((ec, tm), F32), pltpu.VMEM((2, ec, tm), BF16)],
        compiler_params=_params("parallel", "arbitrary"),
        name="pdense",
    )(xn, u, vt, c1, s2, w1, w2, x1, g)


def _layer(x, pos, an_g, w_in, gate_bias, conv_w, wa, wb, wo, fn_g, wq, keys, pu, pv, out_g, *,
           batch, seq, tiles):
    t, d = x.shape
    cw = wa.shape[0]
    aw = wb.shape[0]
    n_heads = aw // HEAD_DIM
    p_heads = keys.shape[0]

    inv_freq = 1.0 / (ROPE_THETA ** (jnp.arange(0, ROT_DIM, 2, dtype=F32) / ROT_DIM))
    invf = jnp.concatenate([inv_freq, inv_freq, jnp.zeros((HEAD_DIM - ROT_DIM,), F32)])[None, :]

    proj = _inproj(x, an_g[None, :], w_in.astype(BF16), tiles["inproj_tm"], tiles["inproj_tn"])
    q, k, v, kmean = _qkvprep(proj, pos, invf, 3 * cw, n_heads)
    kmean = kmean.reshape(batch, seq // MOBA_BLOCK, aw)
    attn = _attn(q, k, v, kmean, batch, seq, n_heads, min(tiles["attn_heads"], n_heads))
    x1, xn = _merge(proj, attn, x, conv_w, gate_bias[None, :], wa.astype(BF16), wb.astype(BF16),
                    wo.astype(BF16), fn_g[None, :], seq, tiles["merge_tm"])
    keys2 = keys.reshape(p_heads * 2, keys.shape[2], keys.shape[3])
    c1, s2, w1, w2 = _pscore(xn, wq.astype(BF16), keys2, p_heads, tiles["pscore_tm"])
    return _pdense(xn, pu.astype(BF16), pv.astype(BF16).T, c1, s2, w1, w2, x1, out_g[None, :],
                   tiles["pdense_tm"], tiles["pdense_ec"])


def _tiles(t, seq):
    return dict(
        inproj_tm=min(1024, t), inproj_tn=1024,
        attn_heads=8,
        merge_tm=min(256, seq),
        pscore_tm=min(512, t),
        pdense_tm=min(512, t), pdense_ec=512,
    )


def kernel(x, positions, attn_norm_g, w_in, gate_bias, conv_w, w_branch_conv, w_branch_attn, w_out,
           ffn_norm_g, w_peer_query, peer_sub_keys, peer_u, peer_v, final_norm_g):
    b, s, d = x.shape
    depth = w_in.shape[0]
    assert depth == 1, "final norm is fused into the last layer's kernel"
    t = b * s
    xt = x.reshape(t, d)
    pos = positions.reshape(t, 1)
    out = _layer(xt, pos, attn_norm_g[0], w_in[0], gate_bias[0], conv_w[0], w_branch_conv[0],
                 w_branch_attn[0], w_out[0], ffn_norm_g[0], w_peer_query[0], peer_sub_keys[0],
                 peer_u[0], peer_v[0], final_norm_g, batch=b, seq=s, tiles=_tiles(t, s))
    return out.reshape(b, s, d)
```

```python
import functools

import jax
import jax.numpy as jnp
from jax import lax
from jax.experimental import pallas as pl
from jax.experimental.pallas import tpu as pltpu

F32 = jnp.float32
BF16 = jnp.bfloat16

HEAD_DIM = 128
ROT_DIM = HEAD_DIM // 4
ROPE_THETA = 500000.0
MOBA_BLOCK = 256
MOBA_TOPK = 3
PEER_TOPK = 16
RMS_EPS = 1e-6

LANES = 128
SUBLANES = 8
VMEM_LIMIT_BYTES = 56 * 1024 * 1024

NEG_BIG = -1e30


def _params(*sem):
    return pltpu.CompilerParams(dimension_semantics=sem, vmem_limit_bytes=VMEM_LIMIT_BYTES)


def _resident(shape):
    zeros = (0,) * len(shape)
    return pl.BlockSpec(shape, lambda *_: zeros, pipeline_mode=pl.Buffered(1))


def _dot_nt_3pass(a, b):
    nt = (((1,), (1,)), ((), ()))
    a_hi, b_hi = a.astype(BF16), b.astype(BF16)
    a_lo = (a - a_hi.astype(F32)).astype(BF16)
    b_lo = (b - b_hi.astype(F32)).astype(BF16)
    dot = lambda x, y: lax.dot_general(x, y, nt, preferred_element_type=F32)
    return dot(a_hi, b_hi) + (dot(a_hi, b_lo) + dot(a_lo, b_hi))


def _rms_norm(x, g):
    ms = jnp.mean(x * x, axis=-1, keepdims=True)
    return x * lax.rsqrt(ms + RMS_EPS) * g


def _inproj_kernel(x_ref, g_ref, w_ref, o_ref, h_ref):
    @pl.when(pl.program_id(1) == 0)
    def _():
        h_ref[...] = _rms_norm(x_ref[...], g_ref[...]).astype(BF16)

    o_ref[...] = jnp.dot(h_ref[...], w_ref[...], preferred_element_type=F32)


def _inproj(x, g, w, tm, tn):
    t, d = x.shape
    n = w.shape[1]
    return pl.pallas_call(
        _inproj_kernel,
        out_shape=jax.ShapeDtypeStruct((t, n), F32),
        grid=(t // tm, n // tn),
        in_specs=[
            pl.BlockSpec((tm, d), lambda i, j: (i, 0)),
            pl.BlockSpec((1, d), lambda i, j: (0, 0)),
            pl.BlockSpec((d, tn), lambda i, j: (0, j)),
        ],
        out_specs=pl.BlockSpec((tm, tn), lambda i, j: (i, j)),
        scratch_shapes=[pltpu.VMEM((tm, d), BF16)],
        compiler_params=_params("parallel", "arbitrary"),
        name="inproj",
    )(x, g, w)


def _qkvprep_kernel(q_ref, k_ref, v_ref, pos_ref, invf_ref, qo_ref, ko_ref, vo_ref, km_ref, *, n_heads):
    half = ROT_DIM // 2
    ang = pos_ref[...].astype(F32) * invf_ref[...]
    cos = jnp.cos(ang)
    sin = jnp.sin(ang)
    lane = lax.broadcasted_iota(jnp.int32, ang.shape, 1)
    sin_lo = jnp.where(lane < half, -sin, 0.0)
    sin_hi = jnp.where((lane >= half) & (lane < ROT_DIM), sin, 0.0)

    def rot(x):
        return x * cos + pltpu.roll(x, HEAD_DIM - half, 1) * sin_lo + pltpu.roll(x, half, 1) * sin_hi

    for h in range(n_heads):
        sl = slice(h * HEAD_DIM, (h + 1) * HEAD_DIM)
        qo_ref[:, sl] = rot(q_ref[:, sl])
        kr = rot(k_ref[:, sl])
        ko_ref[:, sl] = kr.astype(BF16)
        km_ref[0, :, sl] = jnp.mean(kr, axis=0, keepdims=True)
    vo_ref[...] = v_ref[...].astype(BF16)


def _qkvprep(proj, pos, invf, q_col, n_heads):
    t = proj.shape[0]
    w = n_heads * HEAD_DIM
    r = MOBA_BLOCK
    qb = q_col // w
    col = lambda c: pl.BlockSpec((r, w), lambda i: (i, c))
    return pl.pallas_call(
        functools.partial(_qkvprep_kernel, n_heads=n_heads),
        out_shape=(
            jax.ShapeDtypeStruct((t, w), F32),
            jax.ShapeDtypeStruct((t, w), BF16),
            jax.ShapeDtypeStruct((t, w), BF16),
            jax.ShapeDtypeStruct((t // r, 1, w), F32),
        ),
        grid=(t // r,),
        in_specs=[col(qb), col(qb + 1), col(qb + 2),
                  pl.BlockSpec((r, 1), lambda i: (i, 0)),
                  pl.BlockSpec((1, HEAD_DIM), lambda i: (0, 0))],
        out_specs=(
            pl.BlockSpec((r, w), lambda i: (i, 0)),
            pl.BlockSpec((r, w), lambda i: (i, 0)),
            pl.BlockSpec((r, w), lambda i: (i, 0)),
            pl.BlockSpec((1, 1, w), lambda i: (i, 0, 0)),
        ),
        compiler_params=_params("parallel"),
        name="qkvprep",
    )(proj, proj, proj, pos, invf)


_LOG2E = 1.4426950408889634


def _attn_kernel(q_ref, k_ref, v_ref, km_ref, o_ref, *, scale, heads):
    i = pl.program_id(2)
    blk = MOBA_BLOCK
    span = 2 * blk
    hd = HEAD_DIM
    nb = km_ref.shape[1]
    c = scale * _LOG2E
    nt = (((1,), (1,)), ((), ()))
    pair = lax.shift_right_logical(i, 1)
    odd = i - 2 * pair
    ones = jnp.ones((span, LANES), BF16)
    lane = lax.broadcasted_iota(jnp.int32, (span, LANES), 1)
    upper = jnp.where(lax.broadcasted_iota(jnp.int32, (span, LANES), 0) >= blk, 1, 0)
    row = lax.broadcasted_iota(jnp.int32, (blk, span), 0)
    colv = lax.broadcasted_iota(jnp.int32, (blk, span), 1)
    bidx = lax.broadcasted_iota(jnp.int32, (nb, blk), 0)

    def kv_pair(p, hs):
        start = pl.multiple_of(p * span, span)
        onehot = jnp.where(lane == 2 * p + upper, 1.0, 0.0).astype(BF16)
        k_aug = jnp.concatenate([k_ref[pl.ds(start, span), hs], onehot], axis=1)
        v_aug = jnp.concatenate([v_ref[pl.ds(start, span), hs], ones], axis=1)
        return k_aug, v_aug

    q_augs = []
    init = []
    for h in range(heads):
        hs = slice(h * hd, (h + 1) * hd)
        q = q_ref[:, hs]
        gate = _dot_nt_3pass(km_ref[0, :, hs], q)
        gate = jnp.where(bidx < i, gate, -jnp.inf)
        sel = jnp.zeros(gate.shape, F32)
        for r in range(MOBA_TOPK):
            top = jnp.max(gate, axis=0, keepdims=True)
            first = jnp.min(jnp.where(gate == top, bidx, nb), axis=0, keepdims=True)
            hit = bidx == first
            sel = jnp.where(hit & (r < i), 1.0, sel)
            gate = jnp.where(hit, -jnp.inf, gate)
        bias_t = jnp.where((sel > 0.0) | (bidx == i), 0.0, NEG_BIG)
        bias_t = jnp.concatenate([bias_t, jnp.zeros((LANES - nb, blk), F32)], axis=0)
        bias = bias_t.T.astype(BF16)
        q_aug = jnp.concatenate([q.astype(BF16), bias], axis=1)
        q_augs.append(q_aug)

        k_aug, v_aug = kv_pair(pair, hs)
        s = lax.dot_general(q_aug, k_aug, nt, preferred_element_type=F32) * c
        s = jnp.where(colv - row > odd * blk, -jnp.inf, s)
        m0 = jnp.max(s, axis=-1, keepdims=True)
        p = jnp.exp2(s - m0).astype(BF16)
        init.append((m0, jnp.dot(p, v_aug, preferred_element_type=F32)))

    def body(j, carry):
        out = []
        for h in range(heads):
            hs = slice(h * hd, (h + 1) * hd)
            m, acc = carry[h]
            k_aug, v_aug = kv_pair(j, hs)
            s = lax.dot_general(q_augs[h], k_aug, nt, preferred_element_type=F32) * c
            m_new = jnp.maximum(m, jnp.max(s, axis=-1, keepdims=True))
            p = jnp.exp2(s - m_new).astype(BF16)
            acc = jnp.exp2(m - m_new) * acc + jnp.dot(p, v_aug, preferred_element_type=F32)
            out.append((m_new, acc))
        return tuple(out)

    final = lax.fori_loop(0, pair, body, tuple(init))
    for h in range(heads):
        acc = final[h][1]
        o_ref[:, h * hd:(h + 1) * hd] = (acc[:, :hd] / acc[:, hd:]).astype(o_ref.dtype)


def _attn(q, k, v, kmean, batch, seq, n_heads, heads):
    t, w = q.shape
    nq = seq // MOBA_BLOCK
    blk = MOBA_BLOCK
    hw = heads * HEAD_DIM
    assert nq % 2 == 0 and nq <= LANES, "key blocks are visited in pairs; mask columns fit one lane tile"
    return pl.pallas_call(
        functools.partial(_attn_kernel, scale=HEAD_DIM ** -0.5, heads=heads),
        out_shape=jax.ShapeDtypeStruct((t, w), BF16),
        grid=(batch, n_heads // heads, nq),
        in_specs=[
            pl.BlockSpec((blk, hw), lambda b, g, i: (b * nq + i, g)),
            pl.BlockSpec((seq, hw), lambda b, g, i: (b, g)),
            pl.BlockSpec((seq, hw), lambda b, g, i: (b, g)),
            pl.BlockSpec((1, nq, hw), lambda b, g, i: (b, 0, g)),
        ],
        out_specs=pl.BlockSpec((blk, hw), lambda b, g, i: (b * nq + i, g)),
        compiler_params=_params("parallel", "parallel", "arbitrary"),
        name="attn",
    )(q, k, v, kmean)


def _merge_kernel(cb_ref, cc_ref, cx_ref, pc_ref, px_ref, ga_ref, gb_ref, at_ref, x_ref,
                  cw_ref, bias_ref, wa_ref, wb_ref, wo_ref, ng_ref,
                  x1_ref, xn_ref, z_ref, *, seq):
    tm = cb_ref.shape[0]
    d = x_ref.shape[1]
    first = (pl.program_id(0) * tm) % seq == 0
    z = cc_ref[...] * cx_ref[...]
    z_prev = jnp.where(first, 0.0, pc_ref[...] * px_ref[...])
    z_ref[0:SUBLANES, :] = z_prev
    z_ref[SUBLANES:, :] = z
    cw = cw_ref[...]
    conv = (cw[2:3, :] * z
            + cw[1:2, :] * z_ref[SUBLANES - 1:SUBLANES - 1 + tm, :]
            + cw[0:1, :] * z_ref[SUBLANES - 2:SUBLANES - 2 + tm, :])
    u = (cb_ref[...] * conv).astype(BF16)
    y_conv = jnp.dot(u, wa_ref[...], preferred_element_type=F32)
    y_attn = jnp.dot(at_ref[...], wb_ref[...], preferred_element_type=F32)
    bias = bias_ref[...]
    merged = (jax.nn.sigmoid(ga_ref[...] + bias[:, :d]) * y_conv
              + jax.nn.sigmoid(gb_ref[...] + bias[:, d:]) * y_attn)
    x1 = x_ref[...] + jnp.dot(merged.astype(BF16), wo_ref[...], preferred_element_type=F32)
    x1_ref[...] = x1
    xn_ref[...] = _rms_norm(x1, ng_ref[...]).astype(BF16)


def _merge(proj, attn, x, conv_w, gate_bias, wa, wb, wo, ng, seq, tm):
    t, d = x.shape
    cw = wa.shape[0]
    aw = wb.shape[0]
    gcol = (3 * cw + 3 * aw) // d
    halo = lambda c: pl.BlockSpec(
        (SUBLANES, cw), lambda i: (jnp.maximum(i * (tm // SUBLANES) - 1, 0), c))
    return pl.pallas_call(
        functools.partial(_merge_kernel, seq=seq),
        out_shape=(jax.ShapeDtypeStruct((t, d), F32), jax.ShapeDtypeStruct((t, d), BF16)),
        grid=(t // tm,),
        in_specs=[
            pl.BlockSpec((tm, cw), lambda i: (i, 0)),
            pl.BlockSpec((tm, cw), lambda i: (i, 1)),
            pl.BlockSpec((tm, cw), lambda i: (i, 2)),
            halo(1), halo(2),
            pl.BlockSpec((tm, d), lambda i: (i, gcol)),
            pl.BlockSpec((tm, d), lambda i: (i, gcol + 1)),
            pl.BlockSpec((tm, aw), lambda i: (i, 0)),
            pl.BlockSpec((tm, d), lambda i: (i, 0)),
            _resident(conv_w.shape), _resident(gate_bias.shape),
            _resident(wa.shape), _resident(wb.shape), _resident(wo.shape), _resident(ng.shape),
        ],
        out_specs=(pl.BlockSpec((tm, d), lambda i: (i, 0)), pl.BlockSpec((tm, d), lambda i: (i, 0))),
        scratch_shapes=[pltpu.VMEM((tm + SUBLANES, cw), F32)],
        compiler_params=_params("parallel"),
        name="merge",
    )(proj, proj, proj, proj, proj, proj, proj, attn, x, conv_w, gate_bias, wa, wb, wo, ng)


def _sort_network(n):
    pairs = []

    def merge(lo, hi, r):
        step = r * 2
        if step < hi - lo:
            merge(lo, hi, step)
            merge(lo + r, hi, step)
            pairs.extend((i, i + r) for i in range(lo + r, hi - r, step))
        else:
            pairs.append((lo, lo + r))

    def sort(lo, hi):
        if hi - lo >= 1:
            mid = lo + (hi - lo) // 2
            sort(lo, mid)
            sort(mid + 1, hi)
            merge(lo, hi, 1)

    sort(0, n - 1)
    return pairs


def _compare_exchange(xs, i, j):
    a, b = xs[i], xs[j]
    if b is None:
        return
    if a is None:
        xs[i], xs[j] = b, None
        return
    xs[i], xs[j] = jnp.maximum(a, b), jnp.minimum(a, b)


def _sort_desc(xs):
    n = pl.next_power_of_2(len(xs))
    xs = list(xs) + [None] * (n - len(xs))
    for i, j in _sort_network(n):
        _compare_exchange(xs, i, j)
    return xs


def _top_per_column(s):
    n = s.shape[0] // SUBLANES
    xs = _sort_desc([s[SUBLANES * k:SUBLANES * (k + 1), :] for k in range(n)])
    r = SUBLANES // 2
    while r >= 1:
        xs = [jnp.maximum(xs[k], pltpu.roll(xs[n - 1 - k], SUBLANES - r, 0)) for k in range(n)]
        d = n // 2
        while d >= 1:
            for k in range(n):
                if k & d == 0:
                    _compare_exchange(xs, k, k + d)
            d //= 2
        r //= 2
    return xs


def _pscore_kernel(xn_ref, wq_ref, keys_ref, c1_ref, s2_ref, w1_ref, w2_ref, *, n_heads):
    nkeys, half = keys_ref.shape[1], keys_ref.shape[2]
    tm = xn_ref.shape[0]
    topk = PEER_TOPK
    qp = jnp.dot(xn_ref[...], wq_ref[...], preferred_element_type=F32)
    nt = (((1,), (1,)), ((), ()))
    sub = lax.broadcasted_iota(jnp.int32, (SUBLANES, tm), 0)

    tops = [[jnp.zeros((SUBLANES, tm), F32)] * (topk + 1) for _ in range(2)]
    for h in range(n_heads):
        for p, ref in enumerate((c1_ref, s2_ref)):
            c = (2 * h + p) * half
            s = _dot_nt_3pass(keys_ref[2 * h + p], qp[:, c:c + half])
            ref[h] = s
            col = _top_per_column(s)
            below = jnp.max(jnp.where(s < col[topk - 1][0:1, :], s, -jnp.inf), axis=0, keepdims=True)
            col = [pltpu.roll(x, h, 0) if h else x for x in col] + [below]
            tops[p] = [jnp.where(sub == h, col[k], tops[p][k]) for k in range(topk + 1)]
    a, b = tops

    cand = [a[p] + b[q] for p in range(topk + 1) for q in range(topk + 1)
            if (p + 1) * (q + 1) <= topk + 1]
    best = _sort_desc(cand)[:topk + 1]
    z = best[0] * 0.0
    for k in range(topk):
        z = z + jnp.exp(best[k] - best[0])
    tau = 0.5 * (best[topk - 1] + best[topk])
    half_inv_z = 0.5 / z

    for h in range(n_heads):
        row = slice(h, h + 1)
        s1 = c1_ref[h]
        w1_ref[h] = jnp.exp(s1 - a[0][row, :]) * half_inv_z[row, :]
        w2_ref[h] = jnp.exp(s2_ref[h] - b[0][row, :])
        c1_ref[h] = tau[row, :] - s1


def _pscore(xn, wq, keys, n_heads, tm):
    t, d = xn.shape
    nkeys = keys.shape[1]
    assert n_heads == SUBLANES and nkeys == SUBLANES * PEER_TOPK
    big = jax.ShapeDtypeStruct((n_heads, nkeys, t), F32)
    bspec = pl.BlockSpec((n_heads, nkeys, tm), lambda i: (0, 0, i))
    return pl.pallas_call(
        functools.partial(_pscore_kernel, n_heads=n_heads),
        out_shape=(big, big, big, big),
        grid=(t // tm,),
        in_specs=[pl.BlockSpec((tm, d), lambda i: (i, 0)), _resident(wq.shape), _resident(keys.shape)],
        out_specs=(bspec, bspec, bspec, bspec),
        compiler_params=_params("parallel"),
        name="pscore",
    )(xn, wq, keys)


_GELU_C = 0.7978845608028654
_ACT_ROWS = 32
_GATE_ROWS = 32


def _pdense_kernel(xn_ref, u_ref, vt_ref, c1_ref, s2_ref, w1_ref, w2_ref, x1_ref, g_ref,
                   o_ref, acc_ref, gate_ref, a_ref, act_ref, *, n_heads):
    e = pl.program_id(1)
    n_chunks = pl.num_programs(1) - 1
    nkeys = s2_ref.shape[1]
    ec = u_ref.shape[0]
    tm = s2_ref.shape[2]
    nt = (((1,), (1,)), ((), ()))

    @pl.when(e == 0)
    def _():
        acc_ref[...] = jnp.zeros_like(acc_ref)
        act_ref[1] = jnp.zeros(act_ref.shape[1:], act_ref.dtype)

    groups = ec // nkeys
    slot = e % 2
    chunk = jnp.minimum(e, n_chunks - 1)
    d = vt_ref.shape[1]

    def gate_groups(g0, g1):
        gis = range(g0, g1)
        w1_rows = {gi: [w1_ref[h, pl.ds(chunk * groups + gi, 1), :] for h in range(n_heads)] for gi in gis}
        c1_rows = {gi: [c1_ref[h, pl.ds(chunk * groups + gi, 1), :] for h in range(n_heads)] for gi in gis}
        for lt in range(tm // LANES):
            ln = slice(lt * LANES, (lt + 1) * LANES)
            for r0 in range(0, nkeys, _GATE_ROWS):
                rs = slice(r0, r0 + _GATE_ROWS)
                accs = {gi: jnp.zeros((_GATE_ROWS, LANES), F32) for gi in gis}
                for h in range(n_heads):
                    s2t = s2_ref[h, rs, ln]
                    w2t = w2_ref[h, rs, ln]
                    for gi in gis:
                        w = w1_rows[gi][h][:, ln] * w2t
                        accs[gi] = accs[gi] + jnp.where(s2t >= c1_rows[gi][h][:, ln], w, 0.0)
                for gi in gis:
                    gate_ref[gi * nkeys + r0:gi * nkeys + r0 + _GATE_ROWS, ln] = accs[gi]

    def act_rows(r0, r1):
        for r in range(r0, r1, _ACT_ROWS):
            rows = slice(r, r + _ACT_ROWS)
            a = a_ref[rows, :]
            th = jnp.tanh(a * (_GELU_C + (_GELU_C * 0.044715) * (a * a)))
            act_ref[slot, rows, :] = ((a + a * th) * gate_ref[rows, :]).astype(BF16)

    halves = 2
    for m in range(halves):
        rows = slice(m * ec // halves, (m + 1) * ec // halves)
        a_ref[rows, :] = lax.dot_general(u_ref[rows, :], xn_ref[...], nt, preferred_element_type=F32)
        gate_groups(m * groups // halves, (m + 1) * groups // halves)
    for k in range(groups):
        rows = slice(k * d // groups, (k + 1) * d // groups)
        acc_ref[rows, :] += jnp.dot(vt_ref[0, rows, :], act_ref[1 - slot], preferred_element_type=F32)
        act_rows(k * nkeys, (k + 1) * nkeys)

    @pl.when(e == n_chunks)
    def _():
        x2 = x1_ref[...] + acc_ref[...].T
        o_ref[...] = _rms_norm(x2, g_ref[...])


def _pdense(xn, u, vt, c1, s2, w1, w2, x1, g, tm, ec):
    t, d = xn.shape
    n_chunks = u.shape[0] // ec
    n_heads, nkeys, _ = s2.shape
    tok3 = pl.BlockSpec((n_heads, nkeys, tm), lambda i, e: (0, 0, i))
    return pl.pallas_call(
        functools.partial(_pdense_kernel, n_heads=n_heads),
        out_shape=jax.ShapeDtypeStruct((t, d), F32),
        grid=(t // tm, n_chunks + 1),
        in_specs=[
            pl.BlockSpec((tm, d), lambda i, e: (i, 0)),
            pl.BlockSpec((ec, d), lambda i, e: (jnp.minimum(e, n_chunks - 1), 0)),
            pl.BlockSpec((1, d, ec), lambda i, e: (jnp.maximum(e - 1, 0), 0, 0)),
            tok3, tok3, tok3, tok3,
            pl.BlockSpec((tm, d), lambda i, e: (i, 0), pipeline_mode=pl.Buffered(1)),
            pl.BlockSpec((1, d), lambda i, e: (0, 0)),
        ],
        out_specs=pl.BlockSpec((tm, d), lambda i, e: (i, 0)),
        scratch_shapes=[pltpu.VMEM((d, tm), F32), pltpu.VMEM((ec, tm), F32),
                        pltpu.VMEM((ec, tm), F32), pltpu.VMEM((2, ec, tm), BF16)],
        compiler_params=_params("parallel", "arbitrary"),
        name="pdense",
    )(xn, u, vt, c1, s2, w1, w2, x1, g)


def _layer(x, pos, an_g, w_in, gate_bias, conv_w, wa, wb, wo, fn_g, wq, keys, pu, pv, out_g, *,
           batch, seq, tiles):
    t, d = x.shape
    cw = wa.shape[0]
    aw = wb.shape[0]
    n_heads = aw // HEAD_DIM
    p_heads = keys.shape[0]

    inv_freq = 1.0 / (ROPE_THETA ** (jnp.arange(0, ROT_DIM, 2, dtype=F32) / ROT_DIM))
    invf = jnp.concatenate([inv_freq, inv_freq, jnp.zeros((HEAD_DIM - ROT_DIM,), F32)])[None, :]

    proj = _inproj(x, an_g[None, :], w_in.astype(BF16), tiles["inproj_tm"], tiles["inproj_tn"])
    q, k, v, kmean = _qkvprep(proj, pos, invf, 3 * cw, n_heads)
    kmean = kmean.reshape(batch, seq // MOBA_BLOCK, aw)
    attn = _attn(q, k, v, kmean, batch, seq, n_heads, min(tiles["attn_heads"], n_heads))
    x1, xn = _merge(proj, attn, x, conv_w, gate_bias[None, :], wa.astype(BF16), wb.astype(BF16),
                    wo.astype(BF16), fn_g[None, :], seq, tiles["merge_tm"])
    keys2 = keys.reshape(p_heads * 2, keys.shape[2], keys.shape[3])
    c1, s2, w1, w2 = _pscore(xn, wq.astype(BF16), keys2, p_heads, tiles["pscore_tm"])
    ec = tiles["pdense_ec"]
    vt = pv.astype(BF16).reshape(pv.shape[0] // ec, ec, d).transpose(0, 2, 1)
    return _pdense(xn, pu.astype(BF16), vt, c1, s2, w1, w2, x1, out_g[None, :], tiles["pdense_tm"], ec)


def _tiles(t, seq):
    return dict(
        inproj_tm=min(1024, t), inproj_tn=1024,
        attn_heads=8,
        merge_tm=min(256, seq),
        pscore_tm=min(512, t),
        pdense_tm=min(512, t), pdense_ec=512,
    )


def kernel(x, positions, attn_norm_g, w_in, gate_bias, conv_w, w_branch_conv, w_branch_attn, w_out,
           ffn_norm_g, w_peer_query, peer_sub_keys, peer_u, peer_v, final_norm_g):
    b, s, d = x.shape
    depth = w_in.shape[0]
    assert depth == 1, "final norm is fused into the last layer's kernel"
    t = b * s
    xt = x.reshape(t, d)
    pos = positions.reshape(t, 1)
    out = _layer(xt, pos, attn_norm_g[0], w_in[0], gate_bias[0], conv_w[0], w_branch_conv[0],
                 w_branch_attn[0], w_out[0], ffn_norm_g[0], w_peer_query[0], peer_sub_keys[0],
                 peer_u[0], peer_v[0], final_norm_g, batch=b, seq=s, tiles=_tiles(t, s))
    return out.reshape(b, s, d)
```

```python
import functools

import jax
import jax.numpy as jnp
from jax import lax
from jax.experimental import pallas as pl
from jax.experimental.pallas import tpu as pltpu

F32 = jnp.float32
BF16 = jnp.bfloat16

HEAD_DIM = 128
ROT_DIM = HEAD_DIM // 4
ROPE_THETA = 500000.0
MOBA_BLOCK = 256
MOBA_TOPK = 3
PEER_TOPK = 16
RMS_EPS = 1e-6

LANES = 128
SUBLANES = 8
VMEM_LIMIT_BYTES = 56 * 1024 * 1024

NEG_BIG = -1e30


def _params(*sem):
    return pltpu.CompilerParams(dimension_semantics=sem, vmem_limit_bytes=VMEM_LIMIT_BYTES)


def _resident(shape):
    zeros = (0,) * len(shape)
    return pl.BlockSpec(shape, lambda *_: zeros, pipeline_mode=pl.Buffered(1))


def _dot_nt_3pass(a, b):
    nt = (((1,), (1,)), ((), ()))
    a_hi, b_hi = a.astype(BF16), b.astype(BF16)
    a_lo = (a - a_hi.astype(F32)).astype(BF16)
    b_lo = (b - b_hi.astype(F32)).astype(BF16)
    dot = lambda x, y: lax.dot_general(x, y, nt, preferred_element_type=F32)
    return dot(a_hi, b_hi) + (dot(a_hi, b_lo) + dot(a_lo, b_hi))


def _rms_norm(x, g):
    ms = jnp.mean(x * x, axis=-1, keepdims=True)
    return x * lax.rsqrt(ms + RMS_EPS) * g


def _inproj_kernel(x_ref, g_ref, w_ref, o_ref, h_ref):
    @pl.when(pl.program_id(1) == 0)
    def _():
        h_ref[...] = _rms_norm(x_ref[...], g_ref[...]).astype(BF16)

    o_ref[...] = jnp.dot(h_ref[...], w_ref[...], preferred_element_type=F32)


def _inproj(x, g, w, tm, tn):
    t, d = x.shape
    n = w.shape[1]
    return pl.pallas_call(
        _inproj_kernel,
        out_shape=jax.ShapeDtypeStruct((t, n), F32),
        grid=(t // tm, n // tn),
        in_specs=[
            pl.BlockSpec((tm, d), lambda i, j: (i, 0)),
            pl.BlockSpec((1, d), lambda i, j: (0, 0)),
            pl.BlockSpec((d, tn), lambda i, j: (0, j)),
        ],
        out_specs=pl.BlockSpec((tm, tn), lambda i, j: (i, j)),
        scratch_shapes=[pltpu.VMEM((tm, d), BF16)],
        compiler_params=_params("parallel", "arbitrary"),
        name="inproj",
    )(x, g, w)


def _qkvprep_kernel(q_ref, k_ref, v_ref, pos_ref, invf_ref, qo_ref, ko_ref, vo_ref, km_ref, *, n_heads):
    half = ROT_DIM // 2
    ang = pos_ref[...].astype(F32) * invf_ref[...]
    cos = jnp.cos(ang)
    sin = jnp.sin(ang)
    lane = lax.broadcasted_iota(jnp.int32, ang.shape, 1)
    sin_lo = jnp.where(lane < half, -sin, 0.0)
    sin_hi = jnp.where((lane >= half) & (lane < ROT_DIM), sin, 0.0)

    def rot(x):
        return x * cos + pltpu.roll(x, HEAD_DIM - half, 1) * sin_lo + pltpu.roll(x, half, 1) * sin_hi

    for h in range(n_heads):
        sl = slice(h * HEAD_DIM, (h + 1) * HEAD_DIM)
        qo_ref[:, sl] = rot(q_ref[:, sl])
        kr = rot(k_ref[:, sl])
        ko_ref[:, sl] = kr.astype(BF16)
        km_ref[0, :, sl] = jnp.mean(kr, axis=0, keepdims=True)
    vo_ref[...] = v_ref[...].astype(BF16)


def _qkvprep(proj, pos, invf, q_col, n_heads):
    t = proj.shape[0]
    w = n_heads * HEAD_DIM
    r = MOBA_BLOCK
    qb = q_col // w
    col = lambda c: pl.BlockSpec((r, w), lambda i: (i, c))
    return pl.pallas_call(
        functools.partial(_qkvprep_kernel, n_heads=n_heads),
        out_shape=(
            jax.ShapeDtypeStruct((t, w), F32),
            jax.ShapeDtypeStruct((t, w), BF16),
            jax.ShapeDtypeStruct((t, w), BF16),
            jax.ShapeDtypeStruct((t // r, 1, w), F32),
        ),
        grid=(t // r,),
        in_specs=[col(qb), col(qb + 1), col(qb + 2),
                  pl.BlockSpec((r, 1), lambda i: (i, 0)),
                  pl.BlockSpec((1, HEAD_DIM), lambda i: (0, 0))],
        out_specs=(
            pl.BlockSpec((r, w), lambda i: (i, 0)),
            pl.BlockSpec((r, w), lambda i: (i, 0)),
            pl.BlockSpec((r, w), lambda i: (i, 0)),
            pl.BlockSpec((1, 1, w), lambda i: (i, 0, 0)),
        ),
        compiler_params=_params("parallel"),
        name="qkvprep",
    )(proj, proj, proj, pos, invf)


_LOG2E = 1.4426950408889634


def _attn_kernel(q_ref, k_ref, v_ref, km_ref, o_ref, *, scale, heads):
    i = pl.program_id(2)
    blk = MOBA_BLOCK
    span = 2 * blk
    hd = HEAD_DIM
    nb = km_ref.shape[1]
    c = scale * _LOG2E
    nt = (((1,), (1,)), ((), ()))
    pair = lax.shift_right_logical(i, 1)
    odd = i - 2 * pair
    ones = jnp.ones((span, LANES), BF16)
    lane = lax.broadcasted_iota(jnp.int32, (span, LANES), 1)
    upper = jnp.where(lax.broadcasted_iota(jnp.int32, (span, LANES), 0) >= blk, 1, 0)
    row = lax.broadcasted_iota(jnp.int32, (blk, span), 0)
    colv = lax.broadcasted_iota(jnp.int32, (blk, span), 1)
    bidx = lax.broadcasted_iota(jnp.int32, (nb, blk), 0)

    def kv_pair(p, hs):
        start = pl.multiple_of(p * span, span)
        onehot = jnp.where(lane == 2 * p + upper, 1.0, 0.0).astype(BF16)
        k_aug = jnp.concatenate([k_ref[pl.ds(start, span), hs], onehot], axis=1)
        v_aug = jnp.concatenate([v_ref[pl.ds(start, span), hs], ones], axis=1)
        return k_aug, v_aug

    q_augs = []
    init = []
    for h in range(heads):
        hs = slice(h * hd, (h + 1) * hd)
        q = q_ref[:, hs]
        gate = _dot_nt_3pass(km_ref[0, :, hs], q)
        gate = jnp.where(bidx < i, gate, -jnp.inf)
        sel = jnp.zeros(gate.shape, F32)
        for r in range(MOBA_TOPK):
            top = jnp.max(gate, axis=0, keepdims=True)
            first = jnp.min(jnp.where(gate == top, bidx, nb), axis=0, keepdims=True)
            hit = bidx == first
            sel = jnp.where(hit & (r < i), 1.0, sel)
            gate = jnp.where(hit, -jnp.inf, gate)
        bias_t = jnp.where((sel > 0.0) | (bidx == i), 0.0, NEG_BIG)
        bias_t = jnp.concatenate([bias_t, jnp.zeros((LANES - nb, blk), F32)], axis=0)
        bias = bias_t.T.astype(BF16)
        q_aug = jnp.concatenate([q.astype(BF16), bias], axis=1)
        q_augs.append(q_aug)

        k_aug, v_aug = kv_pair(pair, hs)
        s = lax.dot_general(q_aug, k_aug, nt, preferred_element_type=F32) * c
        s = jnp.where(colv - row > odd * blk, -jnp.inf, s)
        m0 = jnp.max(s, axis=-1, keepdims=True)
        p = jnp.exp2(s - m0).astype(BF16)
        init.append((m0, jnp.dot(p, v_aug, preferred_element_type=F32)))

    def body(j, carry):
        out = []
        for h in range(heads):
            hs = slice(h * hd, (h + 1) * hd)
            m, acc = carry[h]
            k_aug, v_aug = kv_pair(j, hs)
            s = lax.dot_general(q_augs[h], k_aug, nt, preferred_element_type=F32) * c
            m_new = jnp.maximum(m, jnp.max(s, axis=-1, keepdims=True))
            p = jnp.exp2(s - m_new).astype(BF16)
            acc = jnp.exp2(m - m_new) * acc + jnp.dot(p, v_aug, preferred_element_type=F32)
            out.append((m_new, acc))
        return tuple(out)

    final = lax.fori_loop(0, pair, body, tuple(init))
    for h in range(heads):
        acc = final[h][1]
        o_ref[:, h * hd:(h + 1) * hd] = (acc[:, :hd] / acc[:, hd:]).astype(o_ref.dtype)


def _attn(q, k, v, kmean, batch, seq, n_heads, heads):
    t, w = q.shape
    nq = seq // MOBA_BLOCK
    blk = MOBA_BLOCK
    hw = heads * HEAD_DIM
    assert nq % 2 == 0 and nq <= LANES, "key blocks are visited in pairs; mask columns fit one lane tile"
    return pl.pallas_call(
        functools.partial(_attn_kernel, scale=HEAD_DIM ** -0.5, heads=heads),
        out_shape=jax.ShapeDtypeStruct((t, w), BF16),
        grid=(batch, n_heads // heads, nq),
        in_specs=[
            pl.BlockSpec((blk, hw), lambda b, g, i: (b * nq + i, g)),
            pl.BlockSpec((seq, hw), lambda b, g, i: (b, g)),
            pl.BlockSpec((seq, hw), lambda b, g, i: (b, g)),
            pl.BlockSpec((1, nq, hw), lambda b, g, i: (b, 0, g)),
        ],
        out_specs=pl.BlockSpec((blk, hw), lambda b, g, i: (b * nq + i, g)),
        compiler_params=_params("parallel", "parallel", "arbitrary"),
        name="attn",
    )(q, k, v, kmean)


def _merge_kernel(cb_ref, cc_ref, cx_ref, pc_ref, px_ref, ga_ref, gb_ref, at_ref, x_ref,
                  cw_ref, bias_ref, wa_ref, wb_ref, wo_ref, ng_ref,
                  x1_ref, xn_ref, z_ref, *, seq):
    tm = cb_ref.shape[0]
    d = x_ref.shape[1]
    first = (pl.program_id(0) * tm) % seq == 0
    z = cc_ref[...] * cx_ref[...]
    z_prev = jnp.where(first, 0.0, pc_ref[...] * px_ref[...])
    z_ref[0:SUBLANES, :] = z_prev
    z_ref[SUBLANES:, :] = z
    cw = cw_ref[...]
    conv = (cw[2:3, :] * z
            + cw[1:2, :] * z_ref[SUBLANES - 1:SUBLANES - 1 + tm, :]
            + cw[0:1, :] * z_ref[SUBLANES - 2:SUBLANES - 2 + tm, :])
    u = (cb_ref[...] * conv).astype(BF16)
    y_conv = jnp.dot(u, wa_ref[...], preferred_element_type=F32)
    y_attn = jnp.dot(at_ref[...], wb_ref[...], preferred_element_type=F32)
    bias = bias_ref[...]
    merged = (jax.nn.sigmoid(ga_ref[...] + bias[:, :d]) * y_conv
              + jax.nn.sigmoid(gb_ref[...] + bias[:, d:]) * y_attn)
    x1 = x_ref[...] + jnp.dot(merged.astype(BF16), wo_ref[...], preferred_element_type=F32)
    x1_ref[...] = x1
    xn_ref[...] = _rms_norm(x1, ng_ref[...]).astype(BF16)


def _merge(proj, attn, x, conv_w, gate_bias, wa, wb, wo, ng, seq, tm):
    t, d = x.shape
    cw = wa.shape[0]
    aw = wb.shape[0]
    gcol = (3 * cw + 3 * aw) // d
    halo = lambda c: pl.BlockSpec(
        (SUBLANES, cw), lambda i: (jnp.maximum(i * (tm // SUBLANES) - 1, 0), c))
    return pl.pallas_call(
        functools.partial(_merge_kernel, seq=seq),
        out_shape=(jax.ShapeDtypeStruct((t, d), F32), jax.ShapeDtypeStruct((t, d), BF16)),
        grid=(t // tm,),
        in_specs=[
            pl.BlockSpec((tm, cw), lambda i: (i, 0)),
            pl.BlockSpec((tm, cw), lambda i: (i, 1)),
            pl.BlockSpec((tm, cw), lambda i: (i, 2)),
            halo(1), halo(2),
            pl.BlockSpec((tm, d), lambda i: (i, gcol)),
            pl.BlockSpec((tm, d), lambda i: (i, gcol + 1)),
            pl.BlockSpec((tm, aw), lambda i: (i, 0)),
            pl.BlockSpec((tm, d), lambda i: (i, 0)),
            _resident(conv_w.shape), _resident(gate_bias.shape),
            _resident(wa.shape), _resident(wb.shape), _resident(wo.shape), _resident(ng.shape),
        ],
        out_specs=(pl.BlockSpec((tm, d), lambda i: (i, 0)), pl.BlockSpec((tm, d), lambda i: (i, 0))),
        scratch_shapes=[pltpu.VMEM((tm + SUBLANES, cw), F32)],
        compiler_params=_params("parallel"),
        name="merge",
    )(proj, proj, proj, proj, proj, proj, proj, attn, x, conv_w, gate_bias, wa, wb, wo, ng)


def _sort_network(n):
    pairs = []

    def merge(lo, hi, r):
        step = r * 2
        if step < hi - lo:
            merge(lo, hi, step)
            merge(lo + r, hi, step)
            pairs.extend((i, i + r) for i in range(lo + r, hi - r, step))
        else:
            pairs.append((lo, lo + r))

    def sort(lo, hi):
        if hi - lo >= 1:
            mid = lo + (hi - lo) // 2
            sort(lo, mid)
            sort(mid + 1, hi)
            merge(lo, hi, 1)

    sort(0, n - 1)
    return pairs


def _compare_exchange(xs, i, j):
    a, b = xs[i], xs[j]
    if b is None:
        return
    if a is None:
        xs[i], xs[j] = b, None
        return
    xs[i], xs[j] = jnp.maximum(a, b), jnp.minimum(a, b)


def _sort_desc(xs):
    n = pl.next_power_of_2(len(xs))
    xs = list(xs) + [None] * (n - len(xs))
    for i, j in _sort_network(n):
        _compare_exchange(xs, i, j)
    return xs


def _top_per_column(s):
    n = s.shape[0] // SUBLANES
    xs = _sort_desc([s[SUBLANES * k:SUBLANES * (k + 1), :] for k in range(n)])
    r = SUBLANES // 2
    while r >= 1:
        xs = [jnp.maximum(xs[k], pltpu.roll(xs[n - 1 - k], SUBLANES - r, 0)) for k in range(n)]
        d = n // 2
        while d >= 1:
            for k in range(n):
                if k & d == 0:
                    _compare_exchange(xs, k, k + d)
            d //= 2
        r //= 2
    return xs


def _pscore_kernel(xn_ref, wq_ref, keys_ref, c1_ref, s2_ref, w1_ref, w2_ref, *, n_heads):
    nkeys, half = keys_ref.shape[1], keys_ref.shape[2]
    tm = xn_ref.shape[0]
    topk = PEER_TOPK
    qp = jnp.dot(xn_ref[...], wq_ref[...], preferred_element_type=F32)
    nt = (((1,), (1,)), ((), ()))
    sub = lax.broadcasted_iota(jnp.int32, (SUBLANES, tm), 0)

    tops = [[jnp.zeros((SUBLANES, tm), F32)] * (topk + 1) for _ in range(2)]
    for h in range(n_heads):
        for p, ref in enumerate((c1_ref, s2_ref)):
            c = (2 * h + p) * half
            s = _dot_nt_3pass(keys_ref[2 * h + p], qp[:, c:c + half])
            ref[h] = s
            col = _top_per_column(s)
            below = jnp.max(jnp.where(s < col[topk - 1][0:1, :], s, -jnp.inf), axis=0, keepdims=True)
            col = [pltpu.roll(x, h, 0) if h else x for x in col] + [below]
            tops[p] = [jnp.where(sub == h, col[k], tops[p][k]) for k in range(topk + 1)]
    a, b = tops

    cand = [a[p] + b[q] for p in range(topk + 1) for q in range(topk + 1)
            if (p + 1) * (q + 1) <= topk + 1]
    best = _sort_desc(cand)[:topk + 1]
    z = best[0] * 0.0
    for k in range(topk):
        z = z + jnp.exp(best[k] - best[0])
    tau = 0.5 * (best[topk - 1] + best[topk])
    half_inv_z = 0.5 / z

    for h in range(n_heads):
        row = slice(h, h + 1)
        s1 = c1_ref[h]
        w1_ref[h] = jnp.exp(s1 - a[0][row, :]) * half_inv_z[row, :]
        w2_ref[h] = jnp.exp(s2_ref[h] - b[0][row, :])
        c1_ref[h] = tau[row, :] - s1


def _pscore(xn, wq, keys, n_heads, tm):
    t, d = xn.shape
    nkeys = keys.shape[1]
    assert n_heads == SUBLANES and nkeys == SUBLANES * PEER_TOPK
    big = jax.ShapeDtypeStruct((n_heads, nkeys, t), F32)
    bspec = pl.BlockSpec((n_heads, nkeys, tm), lambda i: (0, 0, i))
    return pl.pallas_call(
        functools.partial(_pscore_kernel, n_heads=n_heads),
        out_shape=(big, big, big, big),
        grid=(t // tm,),
        in_specs=[pl.BlockSpec((tm, d), lambda i: (i, 0)), _resident(wq.shape), _resident(keys.shape)],
        out_specs=(bspec, bspec, bspec, bspec),
        compiler_params=_params("parallel"),
        name="pscore",
    )(xn, wq, keys)


_GELU_C = 0.7978845608028654
_ACT_ROWS = 32
_GATE_ROWS = 32


def _pdense_kernel(xn_ref, u_ref, v_ref, c1_ref, s2_ref, w1_ref, w2_ref, x1_ref, g_ref,
                   o_ref, gate_ref, a_ref, act_ref, *, n_heads):
    e = pl.program_id(1)
    n_chunks = pl.num_programs(1) - 1
    nkeys = s2_ref.shape[1]
    ec = u_ref.shape[0]
    tm = s2_ref.shape[2]
    nt = (((1,), (1,)), ((), ()))
    tn = (((0,), (0,)), ((), ()))

    @pl.when(e == 0)
    def _():
        o_ref[...] = jnp.zeros_like(o_ref)
        act_ref[1] = jnp.zeros(act_ref.shape[1:], act_ref.dtype)

    groups = ec // nkeys
    slot = e % 2
    chunk = jnp.minimum(e, n_chunks - 1)

    def gate_groups(g0, g1):
        gis = range(g0, g1)
        w1_rows = {gi: [w1_ref[h, 0, gi:gi + 1, :] for h in range(n_heads)] for gi in gis}
        c1_rows = {gi: [c1_ref[h, 0, gi:gi + 1, :] for h in range(n_heads)] for gi in gis}
        for lt in range(tm // LANES):
            ln = slice(lt * LANES, (lt + 1) * LANES)
            for r0 in range(0, nkeys, _GATE_ROWS):
                rs = slice(r0, r0 + _GATE_ROWS)
                accs = {gi: jnp.zeros((_GATE_ROWS, LANES), F32) for gi in gis}
                for h in range(n_heads):
                    s2t = s2_ref[h, rs, ln]
                    w2t = w2_ref[h, rs, ln]
                    for gi in gis:
                        w = w1_rows[gi][h][:, ln] * w2t
                        accs[gi] = accs[gi] + jnp.where(s2t >= c1_rows[gi][h][:, ln], w, 0.0)
                for gi in gis:
                    gate_ref[gi * nkeys + r0:gi * nkeys + r0 + _GATE_ROWS, ln] = accs[gi]

    def act_rows(r0, r1):
        for r in range(r0, r1, _ACT_ROWS):
            rows = slice(r, r + _ACT_ROWS)
            a = a_ref[rows, :]
            th = jnp.tanh(a * (_GELU_C + (_GELU_C * 0.044715) * (a * a)))
            act_ref[slot, rows, :] = ((a + a * th) * gate_ref[rows, :]).astype(BF16)

    halves = 2
    for m in range(halves):
        rows = slice(m * ec // halves, (m + 1) * ec // halves)
        a_ref[rows, :] = lax.dot_general(u_ref[rows, :], xn_ref[...], nt, preferred_element_type=F32)
        gate_groups(m * groups // halves, (m + 1) * groups // halves)
    o_ref[...] += lax.dot_general(act_ref[1 - slot], v_ref[...], tn, preferred_element_type=F32)
    act_rows(0, ec)

    @pl.when(e == n_chunks)
    def _():
        o_ref[...] = _rms_norm(x1_ref[...] + o_ref[...], g_ref[...])


def _pdense(xn, u, v, c1, s2, w1, w2, x1, g, tm, ec):
    t, d = xn.shape
    n_chunks = u.shape[0] // ec
    n_heads, nkeys, _ = s2.shape
    groups = ec // nkeys
    c1 = c1.reshape(n_heads, n_chunks, groups, t)
    w1 = w1.reshape(n_heads, n_chunks, groups, t)
    rows4 = pl.BlockSpec((n_heads, 1, groups, tm), lambda i, e: (0, jnp.minimum(e, n_chunks - 1), 0, i))
    once = pl.Buffered(1)
    return pl.pallas_call(
        functools.partial(_pdense_kernel, n_heads=n_heads),
        out_shape=jax.ShapeDtypeStruct((t, d), F32),
        grid=(t // tm, n_chunks + 1),
        in_specs=[
            pl.BlockSpec((tm, d), lambda i, e: (i, 0), pipeline_mode=once),
            pl.BlockSpec((ec, d), lambda i, e: (jnp.minimum(e, n_chunks - 1), 0)),
            pl.BlockSpec((ec, d), lambda i, e: (jnp.maximum(e - 1, 0), 0)),
            rows4,
            pl.BlockSpec((n_heads, nkeys, tm), lambda i, e: (0, 0, i), pipeline_mode=once),
            rows4,
            pl.BlockSpec((n_heads, nkeys, tm), lambda i, e: (0, 0, i), pipeline_mode=once),
            pl.BlockSpec((tm, d), lambda i, e: (i, 0), pipeline_mode=once),
            pl.BlockSpec((1, d), lambda i, e: (0, 0)),
        ],
        out_specs=pl.BlockSpec((tm, d), lambda i, e: (i, 0), pipeline_mode=once),
        scratch_shapes=[pltpu.VMEM((ec, tm), F32), pltpu.VMEM((ec, tm), F32),
                        pltpu.VMEM((2, ec, tm), BF16)],
        compiler_params=_params("parallel", "arbitrary"),
        name="pdense",
    )(xn, u, v, c1, s2, w1, w2, x1, g)


def _layer(x, pos, an_g, w_in, gate_bias, conv_w, wa, wb, wo, fn_g, wq, keys, pu, pv, out_g, *,
           batch, seq, tiles):
    t, d = x.shape
    cw = wa.shape[0]
    aw = wb.shape[0]
    n_heads = aw // HEAD_DIM
    p_heads = keys.shape[0]

    inv_freq = 1.0 / (ROPE_THETA ** (jnp.arange(0, ROT_DIM, 2, dtype=F32) / ROT_DIM))
    invf = jnp.concatenate([inv_freq, inv_freq, jnp.zeros((HEAD_DIM - ROT_DIM,), F32)])[None, :]

    proj = _inproj(x, an_g[None, :], w_in.astype(BF16), tiles["inproj_tm"], tiles["inproj_tn"])
    q, k, v, kmean = _qkvprep(proj, pos, invf, 3 * cw, n_heads)
    kmean = kmean.reshape(batch, seq // MOBA_BLOCK, aw)
    attn = _attn(q, k, v, kmean, batch, seq, n_heads, min(tiles["attn_heads"], n_heads))
    x1, xn = _merge(proj, attn, x, conv_w, gate_bias[None, :], wa.astype(BF16), wb.astype(BF16),
                    wo.astype(BF16), fn_g[None, :], seq, tiles["merge_tm"])
    keys2 = keys.reshape(p_heads * 2, keys.shape[2], keys.shape[3])
    c1, s2, w1, w2 = _pscore(xn, wq.astype(BF16), keys2, p_heads, tiles["pscore_tm"])
    return _pdense(xn, pu.astype(BF16), pv.astype(BF16), c1, s2, w1, w2, x1, out_g[None, :],
                   tiles["pdense_tm"], tiles["pdense_ec"])


def _tiles(t, seq):
    return dict(
        inproj_tm=min(1024, t), inproj_tn=1024,
        attn_heads=8,
        merge_tm=min(256, seq),
        pscore_tm=min(512, t),
        pdense_tm=min(1024, t), pdense_ec=512,
    )


def kernel(x, positions, attn_norm_g, w_in, gate_bias, conv_w, w_branch_conv, w_branch_attn, w_out,
           ffn_norm_g, w_peer_query, peer_sub_keys, peer_u, peer_v, final_norm_g):
    b, s, d = x.shape
    depth = w_in.shape[0]
    assert depth == 1, "final norm is fused into the last layer's kernel"
    t = b * s
    xt = x.reshape(t, d)
    pos = positions.reshape(t, 1)
    out = _layer(xt, pos, attn_norm_g[0], w_in[0], gate_bias[0], conv_w[0], w_branch_conv[0],
                 w_branch_attn[0], w_out[0], ffn_norm_g[0], w_peer_query[0], peer_sub_keys[0],
                 peer_u[0], peer_v[0], final_norm_g, batch=b, seq=s, tiles=_tiles(t, s))
    return out.reshape(b, s, d)
```

```python
import functools

import jax
import jax.numpy as jnp
from jax import lax
from jax.experimental import pallas as pl
from jax.experimental.pallas import tpu as pltpu

F32 = jnp.float32
BF16 = jnp.bfloat16

HEAD_DIM = 128
ROT_DIM = HEAD_DIM // 4
ROPE_THETA = 500000.0
MOBA_BLOCK = 256
MOBA_TOPK = 3
PEER_TOPK = 16
RMS_EPS = 1e-6

LANES = 128
SUBLANES = 8
VMEM_LIMIT_BYTES = 56 * 1024 * 1024

NEG_BIG = -1e30


def _params(*sem):
    return pltpu.CompilerParams(dimension_semantics=sem, vmem_limit_bytes=VMEM_LIMIT_BYTES)


def _resident(shape):
    zeros = (0,) * len(shape)
    return pl.BlockSpec(shape, lambda *_: zeros, pipeline_mode=pl.Buffered(1))


def _dot_nt_3pass(a, b):
    nt = (((1,), (1,)), ((), ()))
    a_hi, b_hi = a.astype(BF16), b.astype(BF16)
    a_lo = (a - a_hi.astype(F32)).astype(BF16)
    b_lo = (b - b_hi.astype(F32)).astype(BF16)
    dot = lambda x, y: lax.dot_general(x, y, nt, preferred_element_type=F32)
    return dot(a_hi, b_hi) + (dot(a_hi, b_lo) + dot(a_lo, b_hi))


def _rms_norm(x, g):
    ms = jnp.mean(x * x, axis=-1, keepdims=True)
    return x * lax.rsqrt(ms + RMS_EPS) * g


def _inproj_kernel(x_ref, g_ref, w_ref, o_ref, h_ref):
    @pl.when(pl.program_id(1) == 0)
    def _():
        h_ref[...] = _rms_norm(x_ref[...], g_ref[...]).astype(BF16)

    o_ref[...] = jnp.dot(h_ref[...], w_ref[...], preferred_element_type=F32)


def _inproj(x, g, w, tm, tn):
    t, d = x.shape
    n = w.shape[1]
    return pl.pallas_call(
        _inproj_kernel,
        out_shape=jax.ShapeDtypeStruct((t, n), F32),
        grid=(t // tm, n // tn),
        in_specs=[
            pl.BlockSpec((tm, d), lambda i, j: (i, 0)),
            pl.BlockSpec((1, d), lambda i, j: (0, 0)),
            pl.BlockSpec((d, tn), lambda i, j: (0, j)),
        ],
        out_specs=pl.BlockSpec((tm, tn), lambda i, j: (i, j)),
        scratch_shapes=[pltpu.VMEM((tm, d), BF16)],
        compiler_params=_params("parallel", "arbitrary"),
        name="inproj",
    )(x, g, w)


def _qkvprep_kernel(q_ref, k_ref, v_ref, pos_ref, invf_ref, qo_ref, ko_ref, vo_ref, km_ref, *, n_heads):
    half = ROT_DIM // 2
    ang = pos_ref[...].astype(F32) * invf_ref[...]
    cos = jnp.cos(ang)
    sin = jnp.sin(ang)
    lane = lax.broadcasted_iota(jnp.int32, ang.shape, 1)
    sin_lo = jnp.where(lane < half, -sin, 0.0)
    sin_hi = jnp.where((lane >= half) & (lane < ROT_DIM), sin, 0.0)

    def rot(x):
        return x * cos + pltpu.roll(x, HEAD_DIM - half, 1) * sin_lo + pltpu.roll(x, half, 1) * sin_hi

    for h in range(n_heads):
        sl = slice(h * HEAD_DIM, (h + 1) * HEAD_DIM)
        qo_ref[:, sl] = rot(q_ref[:, sl])
        kr = rot(k_ref[:, sl])
        ko_ref[:, sl] = kr.astype(BF16)
        km_ref[0, :, sl] = jnp.mean(kr, axis=0, keepdims=True)
    vo_ref[...] = v_ref[...].astype(BF16)


def _qkvprep(proj, pos, invf, q_col, n_heads):
    t = proj.shape[0]
    w = n_heads * HEAD_DIM
    r = MOBA_BLOCK
    qb = q_col // w
    col = lambda c: pl.BlockSpec((r, w), lambda i: (i, c))
    return pl.pallas_call(
        functools.partial(_qkvprep_kernel, n_heads=n_heads),
        out_shape=(
            jax.ShapeDtypeStruct((t, w), F32),
            jax.ShapeDtypeStruct((t, w), BF16),
            jax.ShapeDtypeStruct((t, w), BF16),
            jax.ShapeDtypeStruct((t // r, 1, w), F32),
        ),
        grid=(t // r,),
        in_specs=[col(qb), col(qb + 1), col(qb + 2),
                  pl.BlockSpec((r, 1), lambda i: (i, 0)),
                  pl.BlockSpec((1, HEAD_DIM), lambda i: (0, 0))],
        out_specs=(
            pl.BlockSpec((r, w), lambda i: (i, 0)),
            pl.BlockSpec((r, w), lambda i: (i, 0)),
            pl.BlockSpec((r, w), lambda i: (i, 0)),
            pl.BlockSpec((1, 1, w), lambda i: (i, 0, 0)),
        ),
        compiler_params=_params("parallel"),
        name="qkvprep",
    )(proj, proj, proj, pos, invf)


_LOG2E = 1.4426950408889634


def _attn_kernel(q_ref, k_ref, v_ref, km_ref, o_ref, *, scale, heads):
    i = pl.program_id(2)
    blk = MOBA_BLOCK
    span = 2 * blk
    hd = HEAD_DIM
    nb = km_ref.shape[1]
    c = scale * _LOG2E
    nt = (((1,), (1,)), ((), ()))
    pair = lax.shift_right_logical(i, 1)
    odd = i - 2 * pair
    ones = jnp.ones((span, LANES), BF16)
    lane = lax.broadcasted_iota(jnp.int32, (span, LANES), 1)
    upper = jnp.where(lax.broadcasted_iota(jnp.int32, (span, LANES), 0) >= blk, 1, 0)
    row = lax.broadcasted_iota(jnp.int32, (blk, span), 0)
    colv = lax.broadcasted_iota(jnp.int32, (blk, span), 1)
    bidx = lax.broadcasted_iota(jnp.int32, (nb, blk), 0)

    def kv_pair(p, hs):
        start = pl.multiple_of(p * span, span)
        onehot = jnp.where(lane == 2 * p + upper, 1.0, 0.0).astype(BF16)
        k_aug = jnp.concatenate([k_ref[pl.ds(start, span), hs], onehot], axis=1)
        v_aug = jnp.concatenate([v_ref[pl.ds(start, span), hs], ones], axis=1)
        return k_aug, v_aug

    q_augs = []
    init = []
    for h in range(heads):
        hs = slice(h * hd, (h + 1) * hd)
        q = q_ref[:, hs]
        gate = _dot_nt_3pass(km_ref[0, :, hs], q)
        gate = jnp.where(bidx < i, gate, -jnp.inf)
        sel = jnp.zeros(gate.shape, F32)
        for r in range(MOBA_TOPK):
            top = jnp.max(gate, axis=0, keepdims=True)
            first = jnp.min(jnp.where(gate == top, bidx, nb), axis=0, keepdims=True)
            hit = bidx == first
            sel = jnp.where(hit & (r < i), 1.0, sel)
            gate = jnp.where(hit, -jnp.inf, gate)
        bias_t = jnp.where((sel > 0.0) | (bidx == i), 0.0, NEG_BIG)
        bias_t = jnp.concatenate([bias_t, jnp.zeros((LANES - nb, blk), F32)], axis=0)
        bias = bias_t.T.astype(BF16)
        q_aug = jnp.concatenate([q.astype(BF16), bias], axis=1)
        q_augs.append(q_aug)

        k_aug, v_aug = kv_pair(pair, hs)
        s = lax.dot_general(q_aug, k_aug, nt, preferred_element_type=F32) * c
        s = jnp.where(colv - row > odd * blk, -jnp.inf, s)
        m0 = jnp.max(s, axis=-1, keepdims=True)
        p = jnp.exp2(s - m0).astype(BF16)
        init.append((m0, jnp.dot(p, v_aug, preferred_element_type=F32)))

    def body(j, carry):
        out = []
        for h in range(heads):
            hs = slice(h * hd, (h + 1) * hd)
            m, acc = carry[h]
            k_aug, v_aug = kv_pair(j, hs)
            s = lax.dot_general(q_augs[h], k_aug, nt, preferred_element_type=F32) * c
            m_new = jnp.maximum(m, jnp.max(s, axis=-1, keepdims=True))
            p = jnp.exp2(s - m_new).astype(BF16)
            acc = jnp.exp2(m - m_new) * acc + jnp.dot(p, v_aug, preferred_element_type=F32)
            out.append((m_new, acc))
        return tuple(out)

    final = lax.fori_loop(0, pair, body, tuple(init))
    for h in range(heads):
        acc = final[h][1]
        o_ref[:, h * hd:(h + 1) * hd] = (acc[:, :hd] / acc[:, hd:]).astype(o_ref.dtype)


def _attn(q, k, v, kmean, batch, seq, n_heads, heads):
    t, w = q.shape
    nq = seq // MOBA_BLOCK
    blk = MOBA_BLOCK
    hw = heads * HEAD_DIM
    assert nq % 2 == 0 and nq <= LANES, "key blocks are visited in pairs; mask columns fit one lane tile"
    return pl.pallas_call(
        functools.partial(_attn_kernel, scale=HEAD_DIM ** -0.5, heads=heads),
        out_shape=jax.ShapeDtypeStruct((t, w), BF16),
        grid=(batch, n_heads // heads, nq),
        in_specs=[
            pl.BlockSpec((blk, hw), lambda b, g, i: (b * nq + i, g)),
            pl.BlockSpec((seq, hw), lambda b, g, i: (b, g)),
            pl.BlockSpec((seq, hw), lambda b, g, i: (b, g)),
            pl.BlockSpec((1, nq, hw), lambda b, g, i: (b, 0, g)),
        ],
        out_specs=pl.BlockSpec((blk, hw), lambda b, g, i: (b * nq + i, g)),
        compiler_params=_params("parallel", "parallel", "arbitrary"),
        name="attn",
    )(q, k, v, kmean)


def _merge_kernel(cb_ref, cc_ref, cx_ref, pc_ref, px_ref, ga_ref, gb_ref, at_ref, x_ref,
                  cw_ref, bias_ref, wa_ref, wb_ref, wo_ref, ng_ref,
                  x1_ref, xn_ref, z_ref, *, seq):
    tm = cb_ref.shape[0]
    d = x_ref.shape[1]
    first = (pl.program_id(0) * tm) % seq == 0
    z = cc_ref[...] * cx_ref[...]
    z_prev = jnp.where(first, 0.0, pc_ref[...] * px_ref[...])
    z_ref[0:SUBLANES, :] = z_prev
    z_ref[SUBLANES:, :] = z
    cw = cw_ref[...]
    conv = (cw[2:3, :] * z
            + cw[1:2, :] * z_ref[SUBLANES - 1:SUBLANES - 1 + tm, :]
            + cw[0:1, :] * z_ref[SUBLANES - 2:SUBLANES - 2 + tm, :])
    u = (cb_ref[...] * conv).astype(BF16)
    y_conv = jnp.dot(u, wa_ref[...], preferred_element_type=F32)
    y_attn = jnp.dot(at_ref[...], wb_ref[...], preferred_element_type=F32)
    bias = bias_ref[...]
    merged = (jax.nn.sigmoid(ga_ref[...] + bias[:, :d]) * y_conv
              + jax.nn.sigmoid(gb_ref[...] + bias[:, d:]) * y_attn)
    x1 = x_ref[...] + jnp.dot(merged.astype(BF16), wo_ref[...], preferred_element_type=F32)
    x1_ref[...] = x1
    xn_ref[...] = _rms_norm(x1, ng_ref[...]).astype(BF16)


def _merge(proj, attn, x, conv_w, gate_bias, wa, wb, wo, ng, seq, tm):
    t, d = x.shape
    cw = wa.shape[0]
    aw = wb.shape[0]
    gcol = (3 * cw + 3 * aw) // d
    halo = lambda c: pl.BlockSpec(
        (SUBLANES, cw), lambda i: (jnp.maximum(i * (tm // SUBLANES) - 1, 0), c))
    return pl.pallas_call(
        functools.partial(_merge_kernel, seq=seq),
        out_shape=(jax.ShapeDtypeStruct((t, d), F32), jax.ShapeDtypeStruct((t, d), BF16)),
        grid=(t // tm,),
        in_specs=[
            pl.BlockSpec((tm, cw), lambda i: (i, 0)),
            pl.BlockSpec((tm, cw), lambda i: (i, 1)),
            pl.BlockSpec((tm, cw), lambda i: (i, 2)),
            halo(1), halo(2),
            pl.BlockSpec((tm, d), lambda i: (i, gcol)),
            pl.BlockSpec((tm, d), lambda i: (i, gcol + 1)),
            pl.BlockSpec((tm, aw), lambda i: (i, 0)),
            pl.BlockSpec((tm, d), lambda i: (i, 0)),
            _resident(conv_w.shape), _resident(gate_bias.shape),
            _resident(wa.shape), _resident(wb.shape), _resident(wo.shape), _resident(ng.shape),
        ],
        out_specs=(pl.BlockSpec((tm, d), lambda i: (i, 0)), pl.BlockSpec((tm, d), lambda i: (i, 0))),
        scratch_shapes=[pltpu.VMEM((tm + SUBLANES, cw), F32)],
        compiler_params=_params("parallel"),
        name="merge",
    )(proj, proj, proj, proj, proj, proj, proj, attn, x, conv_w, gate_bias, wa, wb, wo, ng)


def _sort_network(n):
    pairs = []

    def merge(lo, hi, r):
        step = r * 2
        if step < hi - lo:
            merge(lo, hi, step)
            merge(lo + r, hi, step)
            pairs.extend((i, i + r) for i in range(lo + r, hi - r, step))
        else:
            pairs.append((lo, lo + r))

    def sort(lo, hi):
        if hi - lo >= 1:
            mid = lo + (hi - lo) // 2
            sort(lo, mid)
            sort(mid + 1, hi)
            merge(lo, hi, 1)

    sort(0, n - 1)
    return pairs


def _compare_exchange(xs, i, j):
    a, b = xs[i], xs[j]
    if b is None:
        return
    if a is None:
        xs[i], xs[j] = b, None
        return
    xs[i], xs[j] = jnp.maximum(a, b), jnp.minimum(a, b)


def _sort_desc(xs):
    n = pl.next_power_of_2(len(xs))
    xs = list(xs) + [None] * (n - len(xs))
    for i, j in _sort_network(n):
        _compare_exchange(xs, i, j)
    return xs


def _top_per_column(s):
    n = s.shape[0] // SUBLANES
    xs = _sort_desc([s[SUBLANES * k:SUBLANES * (k + 1), :] for k in range(n)])
    r = SUBLANES // 2
    while r >= 1:
        xs = [jnp.maximum(xs[k], pltpu.roll(xs[n - 1 - k], SUBLANES - r, 0)) for k in range(n)]
        d = n // 2
        while d >= 1:
            for k in range(n):
                if k & d == 0:
                    _compare_exchange(xs, k, k + d)
            d //= 2
        r //= 2
    return xs


def _pscore_kernel(xn_ref, wq_ref, keys_ref, c1_ref, s2_ref, w1_ref, w2_ref, *, n_heads):
    nkeys, half = keys_ref.shape[1], keys_ref.shape[2]
    tm = xn_ref.shape[0]
    topk = PEER_TOPK
    qp = jnp.dot(xn_ref[...], wq_ref[...], preferred_element_type=F32)
    nt = (((1,), (1,)), ((), ()))
    sub = lax.broadcasted_iota(jnp.int32, (SUBLANES, tm), 0)

    tops = [[jnp.zeros((SUBLANES, tm), F32)] * (topk + 1) for _ in range(2)]
    for h in range(n_heads):
        for p, ref in enumerate((c1_ref, s2_ref)):
            c = (2 * h + p) * half
            s = _dot_nt_3pass(keys_ref[2 * h + p], qp[:, c:c + half])
            ref[h] = s
            col = _top_per_column(s)
            below = jnp.max(jnp.where(s < col[topk - 1][0:1, :], s, -jnp.inf), axis=0, keepdims=True)
            col = [pltpu.roll(x, h, 0) if h else x for x in col] + [below]
            tops[p] = [jnp.where(sub == h, col[k], tops[p][k]) for k in range(topk + 1)]
    a, b = tops

    cand = [a[p] + b[q] for p in range(topk + 1) for q in range(topk + 1)
            if (p + 1) * (q + 1) <= topk + 1]
    best = _sort_desc(cand)[:topk + 1]
    z = best[0] * 0.0
    for k in range(topk):
        z = z + jnp.exp(best[k] - best[0])
    tau = 0.5 * (best[topk - 1] + best[topk])
    half_inv_z = 0.5 / z

    for h in range(n_heads):
        row = slice(h, h + 1)
        s1 = c1_ref[h]
        w1_ref[h] = jnp.exp(s1 - a[0][row, :]) * half_inv_z[row, :]
        w2_ref[h] = jnp.exp(s2_ref[h] - b[0][row, :])
        c1_ref[h] = tau[row, :] - s1


def _pscore(xn, wq, keys, n_heads, tm):
    t, d = xn.shape
    nkeys = keys.shape[1]
    assert n_heads == SUBLANES and nkeys == SUBLANES * PEER_TOPK
    big = jax.ShapeDtypeStruct((n_heads, nkeys, t), F32)
    bspec = pl.BlockSpec((n_heads, nkeys, tm), lambda i: (0, 0, i))
    return pl.pallas_call(
        functools.partial(_pscore_kernel, n_heads=n_heads),
        out_shape=(big, big, big, big),
        grid=(t // tm,),
        in_specs=[pl.BlockSpec((tm, d), lambda i: (i, 0)), _resident(wq.shape), _resident(keys.shape)],
        out_specs=(bspec, bspec, bspec, bspec),
        compiler_params=_params("parallel"),
        name="pscore",
    )(xn, wq, keys)


_GELU_C = 0.7978845608028654
_ACT_ROWS = 32
_GATE_ROWS = 32


def _pdense_kernel(xnt_ref, u_ref, v_ref, c1_ref, s2_ref, w1_ref, w2_ref, x1_ref, g_ref,
                   o_ref, gate_ref, a_ref, act_ref, *, n_heads):
    e = pl.program_id(1)
    n_chunks = pl.num_programs(1) - 1
    nkeys = s2_ref.shape[1]
    ec = u_ref.shape[0]
    tm = s2_ref.shape[2]
    nt = (((1,), (1,)), ((), ()))
    tn = (((0,), (0,)), ((), ()))

    @pl.when(e == 0)
    def _():
        o_ref[...] = jnp.zeros_like(o_ref)
        act_ref[1] = jnp.zeros(act_ref.shape[1:], act_ref.dtype)

    groups = ec // nkeys
    slot = e % 2
    chunk = jnp.minimum(e, n_chunks - 1)
    row0 = (chunk * groups) % SUBLANES

    def gate_groups(g0, g1):
        gis = range(g0, g1)
        w1_rows = {gi: [w1_ref[h, pl.ds(row0 + gi, 1), :] for h in range(n_heads)] for gi in gis}
        c1_rows = {gi: [c1_ref[h, pl.ds(row0 + gi, 1), :] for h in range(n_heads)] for gi in gis}
        for lt in range(tm // LANES):
            ln = slice(lt * LANES, (lt + 1) * LANES)
            for r0 in range(0, nkeys, _GATE_ROWS):
                rs = slice(r0, r0 + _GATE_ROWS)
                accs = {gi: jnp.zeros((_GATE_ROWS, LANES), F32) for gi in gis}
                for h in range(n_heads):
                    s2t = s2_ref[h, rs, ln]
                    w2t = w2_ref[h, rs, ln]
                    for gi in gis:
                        w = w1_rows[gi][h][:, ln] * w2t
                        accs[gi] = accs[gi] + jnp.where(s2t >= c1_rows[gi][h][:, ln], w, 0.0)
                for gi in gis:
                    gate_ref[gi * nkeys + r0:gi * nkeys + r0 + _GATE_ROWS, ln] = accs[gi]

    def act_rows(r0, r1):
        for r in range(r0, r1, _ACT_ROWS):
            rows = slice(r, r + _ACT_ROWS)
            a = a_ref[rows, :]
            th = jnp.tanh(a * (_GELU_C + (_GELU_C * 0.044715) * (a * a)))
            act_ref[slot, rows, :] = ((a + a * th) * gate_ref[rows, :]).astype(BF16)

    halves = 2
    for m in range(halves):
        rows = slice(m * ec // halves, (m + 1) * ec // halves)
        a_ref[rows, :] = jnp.dot(u_ref[rows, :], xnt_ref[...], preferred_element_type=F32)
        gate_groups(m * groups // halves, (m + 1) * groups // halves)
    o_ref[...] += lax.dot_general(act_ref[1 - slot], v_ref[...], tn, preferred_element_type=F32)
    act_rows(0, ec)

    @pl.when(e == n_chunks)
    def _():
        o_ref[...] = _rms_norm(x1_ref[...] + o_ref[...], g_ref[...])


def _pdense(xnt, u, v, c1, s2, w1, w2, x1, g, tm, ec):
    d, t = xnt.shape
    n_chunks = u.shape[0] // ec
    n_heads, nkeys, _ = s2.shape
    groups = ec // nkeys
    assert SUBLANES % groups == 0
    rows4 = pl.BlockSpec((n_heads, SUBLANES, tm),
                         lambda i, e: (0, jnp.minimum(e, n_chunks - 1) * groups // SUBLANES, i))
    once = pl.Buffered(1)
    return pl.pallas_call(
        functools.partial(_pdense_kernel, n_heads=n_heads),
        out_shape=jax.ShapeDtypeStruct((t, d), F32),
        grid=(t // tm, n_chunks + 1),
        in_specs=[
            pl.BlockSpec((d, tm), lambda i, e: (0, i), pipeline_mode=once),
            pl.BlockSpec((ec, d), lambda i, e: (jnp.minimum(e, n_chunks - 1), 0)),
            pl.BlockSpec((ec, d), lambda i, e: (jnp.maximum(e - 1, 0), 0)),
            rows4,
            pl.BlockSpec((n_heads, nkeys, tm), lambda i, e: (0, 0, i), pipeline_mode=once),
            rows4,
            pl.BlockSpec((n_heads, nkeys, tm), lambda i, e: (0, 0, i), pipeline_mode=once),
            pl.BlockSpec((tm, d), lambda i, e: (i, 0), pipeline_mode=once),
            pl.BlockSpec((1, d), lambda i, e: (0, 0)),
        ],
        out_specs=pl.BlockSpec((tm, d), lambda i, e: (i, 0), pipeline_mode=once),
        scratch_shapes=[pltpu.VMEM((ec, tm), F32), pltpu.VMEM((ec, tm), F32),
                        pltpu.VMEM((2, ec, tm), BF16)],
        compiler_params=_params("parallel", "arbitrary"),
        name="pdense",
    )(xnt, u, v, c1, s2, w1, w2, x1, g)


def _layer(x, pos, an_g, w_in, gate_bias, conv_w, wa, wb, wo, fn_g, wq, keys, pu, pv, out_g, *,
           batch, seq, tiles):
    t, d = x.shape
    cw = wa.shape[0]
    aw = wb.shape[0]
    n_heads = aw // HEAD_DIM
    p_heads = keys.shape[0]

    inv_freq = 1.0 / (ROPE_THETA ** (jnp.arange(0, ROT_DIM, 2, dtype=F32) / ROT_DIM))
    invf = jnp.concatenate([inv_freq, inv_freq, jnp.zeros((HEAD_DIM - ROT_DIM,), F32)])[None, :]

    proj = _inproj(x, an_g[None, :], w_in.astype(BF16), tiles["inproj_tm"], tiles["inproj_tn"])
    q, k, v, kmean = _qkvprep(proj, pos, invf, 3 * cw, n_heads)
    kmean = kmean.reshape(batch, seq // MOBA_BLOCK, aw)
    attn = _attn(q, k, v, kmean, batch, seq, n_heads, min(tiles["attn_heads"], n_heads))
    x1, xn = _merge(proj, attn, x, conv_w, gate_bias[None, :], wa.astype(BF16), wb.astype(BF16),
                    wo.astype(BF16), fn_g[None, :], seq, tiles["merge_tm"])
    keys2 = keys.reshape(p_heads * 2, keys.shape[2], keys.shape[3])
    c1, s2, w1, w2 = _pscore(xn, wq.astype(BF16), keys2, p_heads, tiles["pscore_tm"])
    return _pdense(xn.T, pu.astype(BF16), pv.astype(BF16), c1, s2, w1, w2, x1, out_g[None, :],
                   tiles["pdense_tm"], tiles["pdense_ec"])


def _tiles(t, seq):
    return dict(
        inproj_tm=min(1024, t), inproj_tn=1024,
        attn_heads=8,
        merge_tm=min(256, seq),
        pscore_tm=min(512, t),
        pdense_tm=min(1024, t), pdense_ec=512,
    )


def kernel(x, positions, attn_norm_g, w_in, gate_bias, conv_w, w_branch_conv, w_branch_attn, w_out,
           ffn_norm_g, w_peer_query, peer_sub_keys, peer_u, peer_v, final_norm_g):
    b, s, d = x.shape
    depth = w_in.shape[0]
    assert depth == 1, "final norm is fused into the last layer's kernel"
    t = b * s
    xt = x.reshape(t, d)
    pos = positions.reshape(t, 1)
    out = _layer(xt, pos, attn_norm_g[0], w_in[0], gate_bias[0], conv_w[0], w_branch_conv[0],
                 w_branch_attn[0], w_out[0], ffn_norm_g[0], w_peer_query[0], peer_sub_keys[0],
                 peer_u[0], peer_v[0], final_norm_g, batch=b, seq=s, tiles=_tiles(t, s))
    return out.reshape(b, s, d)
```

```python
import functools

import jax
import jax.numpy as jnp
from jax import lax
from jax.experimental import pallas as pl
from jax.experimental.pallas import tpu as pltpu

F32 = jnp.float32
BF16 = jnp.bfloat16

HEAD_DIM = 128
ROT_DIM = HEAD_DIM // 4
ROPE_THETA = 500000.0
MOBA_BLOCK = 256
MOBA_TOPK = 3
PEER_TOPK = 16
RMS_EPS = 1e-6

LANES = 128
SUBLANES = 8
VMEM_LIMIT_BYTES = 56 * 1024 * 1024
INPROJ_VMEM_LIMIT_BYTES = 60 * 1024 * 1024

NEG_BIG = -1e30


def _params(*sem, vmem_limit_bytes=VMEM_LIMIT_BYTES):
    return pltpu.CompilerParams(dimension_semantics=sem, vmem_limit_bytes=vmem_limit_bytes)


def _resident(shape):
    zeros = (0,) * len(shape)
    return pl.BlockSpec(shape, lambda *_: zeros, pipeline_mode=pl.Buffered(1))


def _dot_nt_3pass(a, b):
    nt = (((1,), (1,)), ((), ()))
    a_hi, b_hi = a.astype(BF16), b.astype(BF16)
    a_lo = (a - a_hi.astype(F32)).astype(BF16)
    b_lo = (b - b_hi.astype(F32)).astype(BF16)
    dot = lambda x, y: lax.dot_general(x, y, nt, preferred_element_type=F32)
    return dot(a_hi, b_hi) + (dot(a_hi, b_lo) + dot(a_lo, b_hi))


def _rms_norm(x, g):
    ms = jnp.mean(x * x, axis=-1, keepdims=True)
    return x * lax.rsqrt(ms + RMS_EPS) * g


def _inproj_kernel(x_ref, g_ref, w_ref, pos_ref, invf_ref, o_ref, q_ref, k_ref, v_ref, km_ref, h_ref,
                   *, q_tile, n_heads):
    j = pl.program_id(1)

    @pl.when(j == 0)
    def _():
        h_ref[...] = _rms_norm(x_ref[...], g_ref[...]).astype(BF16)

    o_ref[...] = jnp.dot(h_ref[...], w_ref[...], preferred_element_type=F32)

    def rotary():
        half = ROT_DIM // 2
        ang = pos_ref[...].astype(F32) * invf_ref[...]
        cos = jnp.cos(ang)
        sin = jnp.sin(ang)
        lane = lax.broadcasted_iota(jnp.int32, ang.shape, 1)
        sin_lo = jnp.where(lane < half, -sin, 0.0)
        sin_hi = jnp.where((lane >= half) & (lane < ROT_DIM), sin, 0.0)
        return lambda x: (x * cos + pltpu.roll(x, HEAD_DIM - half, 1) * sin_lo
                          + pltpu.roll(x, half, 1) * sin_hi)

    heads = [slice(h * HEAD_DIM, (h + 1) * HEAD_DIM) for h in range(n_heads)]

    @pl.when(j == q_tile)
    def _():
        rot = rotary()
        for hs in heads:
            q_ref[:, hs] = rot(o_ref[:, hs])

    @pl.when(j == q_tile + 1)
    def _():
        rot = rotary()
        for hs in heads:
            kr = rot(o_ref[:, hs])
            k_ref[:, hs] = kr.astype(BF16)
            for b in range(km_ref.shape[0]):
                km_ref[b, :, hs] = jnp.mean(kr[b * MOBA_BLOCK:(b + 1) * MOBA_BLOCK, :], axis=0, keepdims=True)

    @pl.when(j == q_tile + 2)
    def _():
        v_ref[...] = o_ref[...].astype(BF16)


def _inproj(x, g, w, pos, invf, tm, tn, q_col, n_heads):
    t, d = x.shape
    n = w.shape[1]
    aw = n_heads * HEAD_DIM
    assert tn == aw and q_col % tn == 0 and tm % MOBA_BLOCK == 0
    row = lambda i, j: (i, 0)
    return pl.pallas_call(
        functools.partial(_inproj_kernel, q_tile=q_col // tn, n_heads=n_heads),
        out_shape=(
            jax.ShapeDtypeStruct((t, n), F32),
            jax.ShapeDtypeStruct((t, aw), F32),
            jax.ShapeDtypeStruct((t, aw), BF16),
            jax.ShapeDtypeStruct((t, aw), BF16),
            jax.ShapeDtypeStruct((t // MOBA_BLOCK, 1, aw), F32),
        ),
        grid=(t // tm, n // tn),
        in_specs=[
            pl.BlockSpec((tm, d), row),
            pl.BlockSpec((1, d), lambda i, j: (0, 0)),
            pl.BlockSpec((d, tn), lambda i, j: (0, j)),
            pl.BlockSpec((tm, 1), row),
            pl.BlockSpec((1, HEAD_DIM), lambda i, j: (0, 0)),
        ],
        out_specs=(
            pl.BlockSpec((tm, tn), lambda i, j: (i, j)),
            pl.BlockSpec((tm, aw), row),
            pl.BlockSpec((tm, aw), row),
            pl.BlockSpec((tm, aw), row),
            pl.BlockSpec((tm // MOBA_BLOCK, 1, aw), lambda i, j: (i, 0, 0)),
        ),
        scratch_shapes=[pltpu.VMEM((tm, d), BF16)],
        compiler_params=_params("parallel", "arbitrary", vmem_limit_bytes=INPROJ_VMEM_LIMIT_BYTES),
        name="inproj",
    )(x, g, w, pos, invf)


_LOG2E = 1.4426950408889634


def _attn_kernel(q_ref, k_ref, v_ref, km_ref, o_ref, *, scale, heads):
    i = pl.program_id(2)
    blk = MOBA_BLOCK
    span = 2 * blk
    hd = HEAD_DIM
    nb = km_ref.shape[1]
    c = scale * _LOG2E
    nt = (((1,), (1,)), ((), ()))
    pair = lax.shift_right_logical(i, 1)
    odd = i - 2 * pair
    ones = jnp.ones((span, LANES), BF16)
    lane = lax.broadcasted_iota(jnp.int32, (span, LANES), 1)
    upper = jnp.where(lax.broadcasted_iota(jnp.int32, (span, LANES), 0) >= blk, 1, 0)
    row = lax.broadcasted_iota(jnp.int32, (blk, span), 0)
    colv = lax.broadcasted_iota(jnp.int32, (blk, span), 1)
    bidx = lax.broadcasted_iota(jnp.int32, (nb, blk), 0)

    def kv_pair(p, hs):
        start = pl.multiple_of(p * span, span)
        onehot = jnp.where(lane == 2 * p + upper, 1.0, 0.0).astype(BF16)
        k_aug = jnp.concatenate([k_ref[pl.ds(start, span), hs], onehot], axis=1)
        v_aug = jnp.concatenate([v_ref[pl.ds(start, span), hs], ones], axis=1)
        return k_aug, v_aug

    q_augs = []
    init = []
    for h in range(heads):
        hs = slice(h * hd, (h + 1) * hd)
        q = q_ref[:, hs]
        gate = _dot_nt_3pass(km_ref[0, :, hs], q)
        gate = jnp.where(bidx < i, gate, -jnp.inf)
        sel = jnp.zeros(gate.shape, F32)
        for r in range(MOBA_TOPK):
            top = jnp.max(gate, axis=0, keepdims=True)
            first = jnp.min(jnp.where(gate == top, bidx, nb), axis=0, keepdims=True)
            hit = bidx == first
            sel = jnp.where(hit & (r < i), 1.0, sel)
            gate = jnp.where(hit, -jnp.inf, gate)
        bias_t = jnp.where((sel > 0.0) | (bidx == i), 0.0, NEG_BIG)
        bias_t = jnp.concatenate([bias_t, jnp.zeros((LANES - nb, blk), F32)], axis=0)
        bias = bias_t.T.astype(BF16)
        q_aug = jnp.concatenate([q.astype(BF16), bias], axis=1)
        q_augs.append(q_aug)

        k_aug, v_aug = kv_pair(pair, hs)
        s = lax.dot_general(q_aug, k_aug, nt, preferred_element_type=F32) * c
        s = jnp.where(colv - row > odd * blk, -jnp.inf, s)
        m0 = jnp.max(s, axis=-1, keepdims=True)
        p = jnp.exp2(s - m0).astype(BF16)
        init.append((m0, jnp.dot(p, v_aug, preferred_element_type=F32)))

    def body(j, carry):
        out = []
        for h in range(heads):
            hs = slice(h * hd, (h + 1) * hd)
            m, acc = carry[h]
            k_aug, v_aug = kv_pair(j, hs)
            s = lax.dot_general(q_augs[h], k_aug, nt, preferred_element_type=F32) * c
            m_new = jnp.maximum(m, jnp.max(s, axis=-1, keepdims=True))
            p = jnp.exp2(s - m_new).astype(BF16)
            acc = jnp.exp2(m - m_new) * acc + jnp.dot(p, v_aug, preferred_element_type=F32)
            out.append((m_new, acc))
        return tuple(out)

    final = lax.fori_loop(0, pair, body, tuple(init))
    for h in range(heads):
        acc = final[h][1]
        o_ref[:, h * hd:(h + 1) * hd] = (acc[:, :hd] / acc[:, hd:]).astype(o_ref.dtype)


def _attn(q, k, v, kmean, batch, seq, n_heads, heads):
    t, w = q.shape
    nq = seq // MOBA_BLOCK
    blk = MOBA_BLOCK
    hw = heads * HEAD_DIM
    assert nq % 2 == 0 and nq <= LANES, "key blocks are visited in pairs; mask columns fit one lane tile"
    return pl.pallas_call(
        functools.partial(_attn_kernel, scale=HEAD_DIM ** -0.5, heads=heads),
        out_shape=jax.ShapeDtypeStruct((t, w), BF16),
        grid=(batch, n_heads // heads, nq),
        in_specs=[
            pl.BlockSpec((blk, hw), lambda b, g, i: (b * nq + i, g)),
            pl.BlockSpec((seq, hw), lambda b, g, i: (b, g)),
            pl.BlockSpec((seq, hw), lambda b, g, i: (b, g)),
            pl.BlockSpec((1, nq, hw), lambda b, g, i: (b, 0, g)),
        ],
        out_specs=pl.BlockSpec((blk, hw), lambda b, g, i: (b * nq + i, g)),
        compiler_params=_params("parallel", "parallel", "arbitrary"),
        name="attn",
    )(q, k, v, kmean)


def _merge_kernel(cb_ref, cc_ref, cx_ref, pc_ref, px_ref, ga_ref, gb_ref, at_ref, x_ref,
                  cw_ref, bias_ref, wa_ref, wb_ref, wo_ref, ng_ref,
                  x1_ref, xn_ref, xnt_ref, z_ref, *, seq):
    tm = cb_ref.shape[0]
    d = x_ref.shape[1]
    first = (pl.program_id(0) * tm) % seq == 0
    z = cc_ref[...] * cx_ref[...]
    z_prev = jnp.where(first, 0.0, pc_ref[...] * px_ref[...])
    z_ref[0:SUBLANES, :] = z_prev
    z_ref[SUBLANES:, :] = z
    cw = cw_ref[...]
    conv = (cw[2:3, :] * z
            + cw[1:2, :] * z_ref[SUBLANES - 1:SUBLANES - 1 + tm, :]
            + cw[0:1, :] * z_ref[SUBLANES - 2:SUBLANES - 2 + tm, :])
    u = (cb_ref[...] * conv).astype(BF16)
    y_conv = jnp.dot(u, wa_ref[...], preferred_element_type=F32)
    y_attn = jnp.dot(at_ref[...], wb_ref[...], preferred_element_type=F32)
    bias = bias_ref[...]
    merged = (jax.nn.sigmoid(ga_ref[...] + bias[:, :d]) * y_conv
              + jax.nn.sigmoid(gb_ref[...] + bias[:, d:]) * y_attn)
    x1 = x_ref[...] + jnp.dot(merged.astype(BF16), wo_ref[...], preferred_element_type=F32)
    x1_ref[...] = x1
    xn = _rms_norm(x1, ng_ref[...])
    xn_ref[...] = xn.astype(BF16)
    xnt_ref[...] = xn.T.astype(BF16)


def _merge(proj, attn, x, conv_w, gate_bias, wa, wb, wo, ng, seq, tm):
    t, d = x.shape
    cw = wa.shape[0]
    aw = wb.shape[0]
    gcol = (3 * cw + 3 * aw) // d
    halo = lambda c: pl.BlockSpec(
        (SUBLANES, cw), lambda i: (jnp.maximum(i * (tm // SUBLANES) - 1, 0), c))
    return pl.pallas_call(
        functools.partial(_merge_kernel, seq=seq),
        out_shape=(jax.ShapeDtypeStruct((t, d), F32), jax.ShapeDtypeStruct((t, d), BF16),
                   jax.ShapeDtypeStruct((d, t), BF16)),
        grid=(t // tm,),
        in_specs=[
            pl.BlockSpec((tm, cw), lambda i: (i, 0)),
            pl.BlockSpec((tm, cw), lambda i: (i, 1)),
            pl.BlockSpec((tm, cw), lambda i: (i, 2)),
            halo(1), halo(2),
            pl.BlockSpec((tm, d), lambda i: (i, gcol)),
            pl.BlockSpec((tm, d), lambda i: (i, gcol + 1)),
            pl.BlockSpec((tm, aw), lambda i: (i, 0)),
            pl.BlockSpec((tm, d), lambda i: (i, 0)),
            _resident(conv_w.shape), _resident(gate_bias.shape),
            _resident(wa.shape), _resident(wb.shape), _resident(wo.shape), _resident(ng.shape),
        ],
        out_specs=(pl.BlockSpec((tm, d), lambda i: (i, 0)), pl.BlockSpec((tm, d), lambda i: (i, 0)),
                   pl.BlockSpec((d, tm), lambda i: (0, i))),
        scratch_shapes=[pltpu.VMEM((tm + SUBLANES, cw), F32)],
        compiler_params=_params("parallel"),
        name="merge",
    )(proj, proj, proj, proj, proj, proj, proj, attn, x, conv_w, gate_bias, wa, wb, wo, ng)


def _sort_network(n):
    pairs = []

    def merge(lo, hi, r):
        step = r * 2
        if step < hi - lo:
            merge(lo, hi, step)
            merge(lo + r, hi, step)
            pairs.extend((i, i + r) for i in range(lo + r, hi - r, step))
        else:
            pairs.append((lo, lo + r))

    def sort(lo, hi):
        if hi - lo >= 1:
            mid = lo + (hi - lo) // 2
            sort(lo, mid)
            sort(mid + 1, hi)
            merge(lo, hi, 1)

    sort(0, n - 1)
    return pairs


def _compare_exchange(xs, i, j):
    a, b = xs[i], xs[j]
    if b is None:
        return
    if a is None:
        xs[i], xs[j] = b, None
        return
    xs[i], xs[j] = jnp.maximum(a, b), jnp.minimum(a, b)


def _sort_desc(xs):
    n = pl.next_power_of_2(len(xs))
    xs = list(xs) + [None] * (n - len(xs))
    for i, j in _sort_network(n):
        _compare_exchange(xs, i, j)
    return xs


def _top_per_column(s):
    n = s.shape[0] // SUBLANES
    xs = _sort_desc([s[SUBLANES * k:SUBLANES * (k + 1), :] for k in range(n)])
    r = SUBLANES // 2
    while r >= 1:
        xs = [jnp.maximum(xs[k], pltpu.roll(xs[n - 1 - k], SUBLANES - r, 0)) for k in range(n)]
        d = n // 2
        while d >= 1:
            for k in range(n):
                if k & d == 0:
                    _compare_exchange(xs, k, k + d)
            d //= 2
        r //= 2
    return xs


def _pscore_kernel(xn_ref, wq_ref, keys_ref, c1_ref, s2_ref, w1_ref, w2_ref, *, n_heads):
    nkeys, half = keys_ref.shape[1], keys_ref.shape[2]
    tm = xn_ref.shape[0]
    topk = PEER_TOPK
    qp = jnp.dot(xn_ref[...], wq_ref[...], preferred_element_type=F32)
    nt = (((1,), (1,)), ((), ()))
    sub = lax.broadcasted_iota(jnp.int32, (SUBLANES, tm), 0)

    tops = [[jnp.zeros((SUBLANES, tm), F32)] * (topk + 1) for _ in range(2)]
    for h in range(n_heads):
        for p, ref in enumerate((c1_ref, s2_ref)):
            c = (2 * h + p) * half
            s = _dot_nt_3pass(keys_ref[2 * h + p], qp[:, c:c + half])
            ref[h] = s
            col = _top_per_column(s)
            below = jnp.max(jnp.where(s < col[topk - 1][0:1, :], s, -jnp.inf), axis=0, keepdims=True)
            col = [pltpu.roll(x, h, 0) if h else x for x in col] + [below]
            tops[p] = [jnp.where(sub == h, col[k], tops[p][k]) for k in range(topk + 1)]
    a, b = tops

    cand = [a[p] + b[q] for p in range(topk + 1) for q in range(topk + 1)
            if (p + 1) * (q + 1) <= topk + 1]
    best = _sort_desc(cand)[:topk + 1]
    z = best[0] * 0.0
    for k in range(topk):
        z = z + jnp.exp(best[k] - best[0])
    tau = 0.5 * (best[topk - 1] + best[topk])
    half_inv_z = 0.5 / z

    for h in range(n_heads):
        row = slice(h, h + 1)
        s1 = c1_ref[h]
        w1_ref[h] = jnp.exp(s1 - a[0][row, :]) * half_inv_z[row, :]
        w2_ref[h] = jnp.exp(s2_ref[h] - b[0][row, :])
        c1_ref[h] = tau[row, :] - s1


def _pscore(xn, wq, keys, n_heads, tm):
    t, d = xn.shape
    nkeys = keys.shape[1]
    assert n_heads == SUBLANES and nkeys == SUBLANES * PEER_TOPK
    big = jax.ShapeDtypeStruct((n_heads, nkeys, t), F32)
    bspec = pl.BlockSpec((n_heads, nkeys, tm), lambda i: (0, 0, i))
    return pl.pallas_call(
        functools.partial(_pscore_kernel, n_heads=n_heads),
        out_shape=(big, big, big, big),
        grid=(t // tm,),
        in_specs=[pl.BlockSpec((tm, d), lambda i: (i, 0)), _resident(wq.shape), _resident(keys.shape)],
        out_specs=(bspec, bspec, bspec, bspec),
        compiler_params=_params("parallel"),
        name="pscore",
    )(xn, wq, keys)


_GELU_C = 0.7978845608028654
_ACT_ROWS = 32
_GATE_ROWS = 32


def _pdense_kernel(xnt_ref, u_ref, v_ref, c1_ref, s2_ref, w1_ref, w2_ref, x1_ref, g_ref,
                   o_ref, gate_ref, a_ref, act_ref, *, n_heads):
    e = pl.program_id(1)
    n_chunks = pl.num_programs(1) - 1
    nkeys = s2_ref.shape[1]
    ec = u_ref.shape[0]
    tm = s2_ref.shape[2]
    nt = (((1,), (1,)), ((), ()))
    tn = (((0,), (0,)), ((), ()))

    @pl.when(e == 0)
    def _():
        o_ref[...] = jnp.zeros_like(o_ref)
        act_ref[1] = jnp.zeros(act_ref.shape[1:], act_ref.dtype)

    groups = ec // nkeys
    slot = e % 2
    chunk = jnp.minimum(e, n_chunks - 1)
    row0 = (chunk * groups) % SUBLANES

    def gate_groups(g0, g1):
        gis = range(g0, g1)
        w1_rows = {gi: [w1_ref[h, pl.ds(row0 + gi, 1), :] for h in range(n_heads)] for gi in gis}
        c1_rows = {gi: [c1_ref[h, pl.ds(row0 + gi, 1), :] for h in range(n_heads)] for gi in gis}
        for lt in range(tm // LANES):
            ln = slice(lt * LANES, (lt + 1) * LANES)
            for r0 in range(0, nkeys, _GATE_ROWS):
                rs = slice(r0, r0 + _GATE_ROWS)
                accs = {gi: jnp.zeros((_GATE_ROWS, LANES), F32) for gi in gis}
                for h in range(n_heads):
                    s2t = s2_ref[h, rs, ln]
                    w2t = w2_ref[h, rs, ln]
                    for gi in gis:
                        w = w1_rows[gi][h][:, ln] * w2t
                        accs[gi] = accs[gi] + jnp.where(s2t >= c1_rows[gi][h][:, ln], w, 0.0)
                for gi in gis:
                    gate_ref[gi * nkeys + r0:gi * nkeys + r0 + _GATE_ROWS, ln] = accs[gi]

    def act_rows(r0, r1):
        for r in range(r0, r1, _ACT_ROWS):
            rows = slice(r, r + _ACT_ROWS)
            a = a_ref[rows, :]
            th = jnp.tanh(a * (_GELU_C + (_GELU_C * 0.044715) * (a * a)))
            act_ref[slot, rows, :] = ((a + a * th) * gate_ref[rows, :]).astype(BF16)

    halves = 2
    for m in range(halves):
        rows = slice(m * ec // halves, (m + 1) * ec // halves)
        a_ref[rows, :] = jnp.dot(u_ref[rows, :], xnt_ref[...], preferred_element_type=F32)
        gate_groups(m * groups // halves, (m + 1) * groups // halves)
    o_ref[...] += lax.dot_general(act_ref[1 - slot], v_ref[...], tn, preferred_element_type=F32)
    act_rows(0, ec)

    @pl.when(e == n_chunks)
    def _():
        o_ref[...] = _rms_norm(x1_ref[...] + o_ref[...], g_ref[...])


def _pdense(xnt, u, v, c1, s2, w1, w2, x1, g, tm, ec):
    d, t = xnt.shape
    n_chunks = u.shape[0] // ec
    n_heads, nkeys, _ = s2.shape
    groups = ec // nkeys
    assert SUBLANES % groups == 0
    rows4 = pl.BlockSpec((n_heads, SUBLANES, tm),
                         lambda i, e: (0, jnp.minimum(e, n_chunks - 1) * groups // SUBLANES, i))
    once = pl.Buffered(1)
    return pl.pallas_call(
        functools.partial(_pdense_kernel, n_heads=n_heads),
        out_shape=jax.ShapeDtypeStruct((t, d), F32),
        grid=(t // tm, n_chunks + 1),
        in_specs=[
            pl.BlockSpec((d, tm), lambda i, e: (0, i), pipeline_mode=once),
            pl.BlockSpec((ec, d), lambda i, e: (jnp.minimum(e, n_chunks - 1), 0)),
            pl.BlockSpec((ec, d), lambda i, e: (jnp.maximum(e - 1, 0), 0)),
            rows4,
            pl.BlockSpec((n_heads, nkeys, tm), lambda i, e: (0, 0, i), pipeline_mode=once),
            rows4,
            pl.BlockSpec((n_heads, nkeys, tm), lambda i, e: (0, 0, i), pipeline_mode=once),
            pl.BlockSpec((tm, d), lambda i, e: (i, 0), pipeline_mode=once),
            pl.BlockSpec((1, d), lambda i, e: (0, 0)),
        ],
        out_specs=pl.BlockSpec((tm, d), lambda i, e: (i, 0), pipeline_mode=once),
        scratch_shapes=[pltpu.VMEM((ec, tm), F32), pltpu.VMEM((ec, tm), F32),
                        pltpu.VMEM((2, ec, tm), BF16)],
        compiler_params=_params("parallel", "arbitrary"),
        name="pdense",
    )(xnt, u, v, c1, s2, w1, w2, x1, g)


def _layer(x, pos, an_g, w_in, gate_bias, conv_w, wa, wb, wo, fn_g, wq, keys, pu, pv, out_g, *,
           batch, seq, tiles):
    t, d = x.shape
    cw = wa.shape[0]
    aw = wb.shape[0]
    n_heads = aw // HEAD_DIM
    p_heads = keys.shape[0]

    inv_freq = 1.0 / (ROPE_THETA ** (jnp.arange(0, ROT_DIM, 2, dtype=F32) / ROT_DIM))
    invf = jnp.concatenate([inv_freq, inv_freq, jnp.zeros((HEAD_DIM - ROT_DIM,), F32)])[None, :]

    proj, q, k, v, kmean = _inproj(x, an_g[None, :], w_in.astype(BF16), pos, invf,
                                   tiles["inproj_tm"], aw, 3 * cw, n_heads)
    kmean = kmean.reshape(batch, seq // MOBA_BLOCK, aw)
    attn = _attn(q, k, v, kmean, batch, seq, n_heads, min(tiles["attn_heads"], n_heads))
    x1, xn, xnt = _merge(proj, attn, x, conv_w, gate_bias[None, :], wa.astype(BF16), wb.astype(BF16),
                    wo.astype(BF16), fn_g[None, :], seq, tiles["merge_tm"])
    keys2 = keys.reshape(p_heads * 2, keys.shape[2], keys.shape[3])
    c1, s2, w1, w2 = _pscore(xn, wq.astype(BF16), keys2, p_heads, tiles["pscore_tm"])
    return _pdense(xnt, pu.astype(BF16), pv.astype(BF16), c1, s2, w1, w2, x1, out_g[None, :],
                   tiles["pdense_tm"], tiles["pdense_ec"])


def _tiles(t, seq):
    return dict(
        inproj_tm=min(1024, t),
        attn_heads=8,
        merge_tm=min(256, seq),
        pscore_tm=min(512, t),
        pdense_tm=min(1024, t), pdense_ec=512,
    )


def kernel(x, positions, attn_norm_g, w_in, gate_bias, conv_w, w_branch_conv, w_branch_attn, w_out,
           ffn_norm_g, w_peer_query, peer_sub_keys, peer_u, peer_v, final_norm_g):
    b, s, d = x.shape
    depth = w_in.shape[0]
    assert depth == 1, "final norm is fused into the last layer's kernel"
    t = b * s
    xt = x.reshape(t, d)
    pos = positions.reshape(t, 1)
    out = _layer(xt, pos, attn_norm_g[0], w_in[0], gate_bias[0], conv_w[0], w_branch_conv[0],
                 w_branch_attn[0], w_out[0], ffn_norm_g[0], w_peer_query[0], peer_sub_keys[0],
                 peer_u[0], peer_v[0], final_norm_g, batch=b, seq=s, tiles=_tiles(t, s))
    return out.reshape(b, s, d)
```

```python
import functools

import jax
import jax.numpy as jnp
from jax import lax
from jax.experimental import pallas as pl
from jax.experimental.pallas import tpu as pltpu

F32 = jnp.float32
BF16 = jnp.bfloat16

HEAD_DIM = 128
ROT_DIM = HEAD_DIM // 4
ROPE_THETA = 500000.0
MOBA_BLOCK = 256
MOBA_TOPK = 3
PEER_TOPK = 16
RMS_EPS = 1e-6

LANES = 128
SUBLANES = 8
VMEM_LIMIT_BYTES = 56 * 1024 * 1024
INPROJ_VMEM_LIMIT_BYTES = 60 * 1024 * 1024
PDENSE_VMEM_LIMIT_BYTES = 60 * 1024 * 1024

NEG_BIG = -1e30


def _params(*sem, vmem_limit_bytes=VMEM_LIMIT_BYTES):
    return pltpu.CompilerParams(dimension_semantics=sem, vmem_limit_bytes=vmem_limit_bytes)


def _resident(shape):
    zeros = (0,) * len(shape)
    return pl.BlockSpec(shape, lambda *_: zeros, pipeline_mode=pl.Buffered(1))


def _dot_nt_3pass(a, b):
    nt = (((1,), (1,)), ((), ()))
    a_hi, b_hi = a.astype(BF16), b.astype(BF16)
    a_lo = (a - a_hi.astype(F32)).astype(BF16)
    b_lo = (b - b_hi.astype(F32)).astype(BF16)
    dot = lambda x, y: lax.dot_general(x, y, nt, preferred_element_type=F32)
    return dot(a_hi, b_hi) + (dot(a_hi, b_lo) + dot(a_lo, b_hi))


def _rms_norm(x, g):
    ms = jnp.mean(x * x, axis=-1, keepdims=True)
    return x * lax.rsqrt(ms + RMS_EPS) * g


def _inproj_kernel(x_ref, g_ref, w_ref, pos_ref, invf_ref, o_ref, q_ref, k_ref, v_ref, km_ref, h_ref,
                   *, q_tile, n_heads):
    j = pl.program_id(1)

    @pl.when(j == 0)
    def _():
        h_ref[...] = _rms_norm(x_ref[...], g_ref[...]).astype(BF16)

    o_ref[...] = jnp.dot(h_ref[...], w_ref[...], preferred_element_type=F32)

    def rotary():
        half = ROT_DIM // 2
        ang = pos_ref[...].astype(F32) * invf_ref[...]
        cos = jnp.cos(ang)
        sin = jnp.sin(ang)
        lane = lax.broadcasted_iota(jnp.int32, ang.shape, 1)
        sin_lo = jnp.where(lane < half, -sin, 0.0)
        sin_hi = jnp.where((lane >= half) & (lane < ROT_DIM), sin, 0.0)
        return lambda x: (x * cos + pltpu.roll(x, HEAD_DIM - half, 1) * sin_lo
                          + pltpu.roll(x, half, 1) * sin_hi)

    heads = [slice(h * HEAD_DIM, (h + 1) * HEAD_DIM) for h in range(n_heads)]

    @pl.when(j == q_tile)
    def _():
        rot = rotary()
        for hs in heads:
            q_ref[:, hs] = rot(o_ref[:, hs])

    @pl.when(j == q_tile + 1)
    def _():
        rot = rotary()
        for hs in heads:
            kr = rot(o_ref[:, hs])
            k_ref[:, hs] = kr.astype(BF16)
            for b in range(km_ref.shape[0]):
                km_ref[b, :, hs] = jnp.mean(kr[b * MOBA_BLOCK:(b + 1) * MOBA_BLOCK, :], axis=0, keepdims=True)

    @pl.when(j == q_tile + 2)
    def _():
        v_ref[...] = o_ref[...].astype(BF16)


def _inproj(x, g, w, pos, invf, tm, tn, q_col, n_heads):
    t, d = x.shape
    n = w.shape[1]
    aw = n_heads * HEAD_DIM
    assert tn == aw and q_col % tn == 0 and tm % MOBA_BLOCK == 0
    row = lambda i, j: (i, 0)
    return pl.pallas_call(
        functools.partial(_inproj_kernel, q_tile=q_col // tn, n_heads=n_heads),
        out_shape=(
            jax.ShapeDtypeStruct((t, n), F32),
            jax.ShapeDtypeStruct((t, aw), F32),
            jax.ShapeDtypeStruct((t, aw), BF16),
            jax.ShapeDtypeStruct((t, aw), BF16),
            jax.ShapeDtypeStruct((t // MOBA_BLOCK, 1, aw), F32),
        ),
        grid=(t // tm, n // tn),
        in_specs=[
            pl.BlockSpec((tm, d), row),
            pl.BlockSpec((1, d), lambda i, j: (0, 0)),
            pl.BlockSpec((d, tn), lambda i, j: (0, j)),
            pl.BlockSpec((tm, 1), row),
            pl.BlockSpec((1, HEAD_DIM), lambda i, j: (0, 0)),
        ],
        out_specs=(
            pl.BlockSpec((tm, tn), lambda i, j: (i, j)),
            pl.BlockSpec((tm, aw), row),
            pl.BlockSpec((tm, aw), row),
            pl.BlockSpec((tm, aw), row),
            pl.BlockSpec((tm // MOBA_BLOCK, 1, aw), lambda i, j: (i, 0, 0)),
        ),
        scratch_shapes=[pltpu.VMEM((tm, d), BF16)],
        compiler_params=_params("parallel", "arbitrary", vmem_limit_bytes=INPROJ_VMEM_LIMIT_BYTES),
        name="inproj",
    )(x, g, w, pos, invf)


_LOG2E = 1.4426950408889634


def _attn_kernel(q_ref, k_ref, v_ref, km_ref, o_ref, *, scale, heads):
    i = pl.program_id(2)
    blk = MOBA_BLOCK
    span = 2 * blk
    hd = HEAD_DIM
    nb = km_ref.shape[1]
    c = scale * _LOG2E
    nt = (((1,), (1,)), ((), ()))
    pair = lax.shift_right_logical(i, 1)
    odd = i - 2 * pair
    ones = jnp.ones((span, LANES), BF16)
    lane = lax.broadcasted_iota(jnp.int32, (span, LANES), 1)
    upper = jnp.where(lax.broadcasted_iota(jnp.int32, (span, LANES), 0) >= blk, 1, 0)
    row = lax.broadcasted_iota(jnp.int32, (blk, span), 0)
    colv = lax.broadcasted_iota(jnp.int32, (blk, span), 1)
    bidx = lax.broadcasted_iota(jnp.int32, (nb, blk), 0)

    def kv_pair(p, hs):
        start = pl.multiple_of(p * span, span)
        onehot = jnp.where(lane == 2 * p + upper, 1.0, 0.0).astype(BF16)
        k_aug = jnp.concatenate([k_ref[pl.ds(start, span), hs], onehot], axis=1)
        v_aug = jnp.concatenate([v_ref[pl.ds(start, span), hs], ones], axis=1)
        return k_aug, v_aug

    q_augs = []
    init = []
    for h in range(heads):
        hs = slice(h * hd, (h + 1) * hd)
        q = q_ref[:, hs]
        gate = _dot_nt_3pass(km_ref[0, :, hs], q)
        gate = jnp.where(bidx < i, gate, -jnp.inf)
        sel = jnp.zeros(gate.shape, F32)
        for r in range(MOBA_TOPK):
            top = jnp.max(gate, axis=0, keepdims=True)
            first = jnp.min(jnp.where(gate == top, bidx, nb), axis=0, keepdims=True)
            hit = bidx == first
            sel = jnp.where(hit & (r < i), 1.0, sel)
            gate = jnp.where(hit, -jnp.inf, gate)
        bias_t = jnp.where((sel > 0.0) | (bidx == i), 0.0, NEG_BIG)
        bias_t = jnp.concatenate([bias_t, jnp.zeros((LANES - nb, blk), F32)], axis=0)
        bias = bias_t.T.astype(BF16)
        q_aug = jnp.concatenate([q.astype(BF16), bias], axis=1)
        q_augs.append(q_aug)

        k_aug, v_aug = kv_pair(pair, hs)
        s = lax.dot_general(q_aug, k_aug, nt, preferred_element_type=F32) * c
        s = jnp.where(colv - row > odd * blk, -jnp.inf, s)
        m0 = jnp.max(s, axis=-1, keepdims=True)
        p = jnp.exp2(s - m0).astype(BF16)
        init.append((m0, jnp.dot(p, v_aug, preferred_element_type=F32)))

    def body(j, carry):
        out = []
        for h in range(heads):
            hs = slice(h * hd, (h + 1) * hd)
            m, acc = carry[h]
            k_aug, v_aug = kv_pair(j, hs)
            s = lax.dot_general(q_augs[h], k_aug, nt, preferred_element_type=F32) * c
            m_new = jnp.maximum(m, jnp.max(s, axis=-1, keepdims=True))
            p = jnp.exp2(s - m_new).astype(BF16)
            acc = jnp.exp2(m - m_new) * acc + jnp.dot(p, v_aug, preferred_element_type=F32)
            out.append((m_new, acc))
        return tuple(out)

    final = lax.fori_loop(0, pair, body, tuple(init))
    for h in range(heads):
        acc = final[h][1]
        o_ref[:, h * hd:(h + 1) * hd] = (acc[:, :hd] / acc[:, hd:]).astype(o_ref.dtype)


def _attn(q, k, v, kmean, batch, seq, n_heads, heads):
    t, w = q.shape
    nq = seq // MOBA_BLOCK
    blk = MOBA_BLOCK
    hw = heads * HEAD_DIM
    assert nq % 2 == 0 and nq <= LANES, "key blocks are visited in pairs; mask columns fit one lane tile"
    return pl.pallas_call(
        functools.partial(_attn_kernel, scale=HEAD_DIM ** -0.5, heads=heads),
        out_shape=jax.ShapeDtypeStruct((t, w), BF16),
        grid=(batch, n_heads // heads, nq),
        in_specs=[
            pl.BlockSpec((blk, hw), lambda b, g, i: (b * nq + i, g)),
            pl.BlockSpec((seq, hw), lambda b, g, i: (b, g)),
            pl.BlockSpec((seq, hw), lambda b, g, i: (b, g)),
            pl.BlockSpec((1, nq, hw), lambda b, g, i: (b, 0, g)),
        ],
        out_specs=pl.BlockSpec((blk, hw), lambda b, g, i: (b * nq + i, g)),
        compiler_params=_params("parallel", "parallel", "arbitrary"),
        name="attn",
    )(q, k, v, kmean)


def _merge_kernel(cb_ref, cc_ref, cx_ref, pc_ref, px_ref, ga_ref, gb_ref, at_ref, x_ref,
                  cw_ref, bias_ref, wa_ref, wb_ref, wo_ref, ng_ref,
                  x1_ref, xn_ref, xnt_ref, z_ref, *, seq):
    tm = cb_ref.shape[0]
    d = x_ref.shape[1]
    first = (pl.program_id(0) * tm) % seq == 0
    z = cc_ref[...] * cx_ref[...]
    z_prev = jnp.where(first, 0.0, pc_ref[...] * px_ref[...])
    z_ref[0:SUBLANES, :] = z_prev
    z_ref[SUBLANES:, :] = z
    cw = cw_ref[...]
    conv = (cw[2:3, :] * z
            + cw[1:2, :] * z_ref[SUBLANES - 1:SUBLANES - 1 + tm, :]
            + cw[0:1, :] * z_ref[SUBLANES - 2:SUBLANES - 2 + tm, :])
    u = (cb_ref[...] * conv).astype(BF16)
    y_conv = jnp.dot(u, wa_ref[...], preferred_element_type=F32)
    y_attn = jnp.dot(at_ref[...], wb_ref[...], preferred_element_type=F32)
    bias = bias_ref[...]
    merged = (jax.nn.sigmoid(ga_ref[...] + bias[:, :d]) * y_conv
              + jax.nn.sigmoid(gb_ref[...] + bias[:, d:]) * y_attn)
    x1 = x_ref[...] + jnp.dot(merged.astype(BF16), wo_ref[...], preferred_element_type=F32)
    x1_ref[...] = x1
    xn = _rms_norm(x1, ng_ref[...])
    xn_ref[...] = xn.astype(BF16)
    xnt_ref[...] = xn.T.astype(BF16)


def _merge(proj, attn, x, conv_w, gate_bias, wa, wb, wo, ng, seq, tm):
    t, d = x.shape
    cw = wa.shape[0]
    aw = wb.shape[0]
    gcol = (3 * cw + 3 * aw) // d
    halo = lambda c: pl.BlockSpec(
        (SUBLANES, cw), lambda i: (jnp.maximum(i * (tm // SUBLANES) - 1, 0), c))
    return pl.pallas_call(
        functools.partial(_merge_kernel, seq=seq),
        out_shape=(jax.ShapeDtypeStruct((t, d), F32), jax.ShapeDtypeStruct((t, d), BF16),
                   jax.ShapeDtypeStruct((d, t), BF16)),
        grid=(t // tm,),
        in_specs=[
            pl.BlockSpec((tm, cw), lambda i: (i, 0)),
            pl.BlockSpec((tm, cw), lambda i: (i, 1)),
            pl.BlockSpec((tm, cw), lambda i: (i, 2)),
            halo(1), halo(2),
            pl.BlockSpec((tm, d), lambda i: (i, gcol)),
            pl.BlockSpec((tm, d), lambda i: (i, gcol + 1)),
            pl.BlockSpec((tm, aw), lambda i: (i, 0)),
            pl.BlockSpec((tm, d), lambda i: (i, 0)),
            _resident(conv_w.shape), _resident(gate_bias.shape),
            _resident(wa.shape), _resident(wb.shape), _resident(wo.shape), _resident(ng.shape),
        ],
        out_specs=(pl.BlockSpec((tm, d), lambda i: (i, 0)), pl.BlockSpec((tm, d), lambda i: (i, 0)),
                   pl.BlockSpec((d, tm), lambda i: (0, i))),
        scratch_shapes=[pltpu.VMEM((tm + SUBLANES, cw), F32)],
        compiler_params=_params("parallel"),
        name="merge",
    )(proj, proj, proj, proj, proj, proj, proj, attn, x, conv_w, gate_bias, wa, wb, wo, ng)


def _sort_network(n):
    pairs = []

    def merge(lo, hi, r):
        step = r * 2
        if step < hi - lo:
            merge(lo, hi, step)
            merge(lo + r, hi, step)
            pairs.extend((i, i + r) for i in range(lo + r, hi - r, step))
        else:
            pairs.append((lo, lo + r))

    def sort(lo, hi):
        if hi - lo >= 1:
            mid = lo + (hi - lo) // 2
            sort(lo, mid)
            sort(mid + 1, hi)
            merge(lo, hi, 1)

    sort(0, n - 1)
    return pairs


def _compare_exchange(xs, i, j):
    a, b = xs[i], xs[j]
    if b is None:
        return
    if a is None:
        xs[i], xs[j] = b, None
        return
    xs[i], xs[j] = jnp.maximum(a, b), jnp.minimum(a, b)


def _sort_desc(xs):
    n = pl.next_power_of_2(len(xs))
    xs = list(xs) + [None] * (n - len(xs))
    for i, j in _sort_network(n):
        _compare_exchange(xs, i, j)
    return xs


def _top_per_column(s):
    n = s.shape[0] // SUBLANES
    xs = _sort_desc([s[SUBLANES * k:SUBLANES * (k + 1), :] for k in range(n)])
    r = SUBLANES // 2
    while r >= 1:
        xs = [jnp.maximum(xs[k], pltpu.roll(xs[n - 1 - k], SUBLANES - r, 0)) for k in range(n)]
        d = n // 2
        while d >= 1:
            for k in range(n):
                if k & d == 0:
                    _compare_exchange(xs, k, k + d)
            d //= 2
        r //= 2
    return xs


def _pscore_kernel(xn_ref, wq_ref, keys_ref, c1_ref, s2_ref, w1_ref, w2_ref, *, n_heads):
    nkeys, half = keys_ref.shape[1], keys_ref.shape[2]
    tm = xn_ref.shape[0]
    topk = PEER_TOPK
    qp = jnp.dot(xn_ref[...], wq_ref[...], preferred_element_type=F32)
    nt = (((1,), (1,)), ((), ()))
    sub = lax.broadcasted_iota(jnp.int32, (SUBLANES, tm), 0)

    tops = [[jnp.zeros((SUBLANES, tm), F32)] * (topk + 1) for _ in range(2)]
    for h in range(n_heads):
        for p, ref in enumerate((c1_ref, s2_ref)):
            c = (2 * h + p) * half
            s = _dot_nt_3pass(keys_ref[2 * h + p], qp[:, c:c + half])
            ref[h] = s
            col = _top_per_column(s)
            below = jnp.max(jnp.where(s < col[topk - 1][0:1, :], s, -jnp.inf), axis=0, keepdims=True)
            col = [pltpu.roll(x, h, 0) if h else x for x in col] + [below]
            tops[p] = [jnp.where(sub == h, col[k], tops[p][k]) for k in range(topk + 1)]
    a, b = tops

    cand = [a[p] + b[q] for p in range(topk + 1) for q in range(topk + 1)
            if (p + 1) * (q + 1) <= topk + 1]
    best = _sort_desc(cand)[:topk + 1]
    z = best[0] * 0.0
    for k in range(topk):
        z = z + jnp.exp(best[k] - best[0])
    tau = 0.5 * (best[topk - 1] + best[topk])
    half_inv_z = 0.5 / z

    for h in range(n_heads):
        row = slice(h, h + 1)
        s1 = c1_ref[h]
        w1_ref[h] = jnp.exp(s1 - a[0][row, :]) * half_inv_z[row, :]
        w2_ref[h] = jnp.exp(s2_ref[h] - b[0][row, :])
        c1_ref[h] = tau[row, :] - s1


def _pscore(xn, wq, keys, n_heads, tm):
    t, d = xn.shape
    nkeys = keys.shape[1]
    assert n_heads == SUBLANES and nkeys == SUBLANES * PEER_TOPK
    big = jax.ShapeDtypeStruct((n_heads, nkeys, t), F32)
    bspec = pl.BlockSpec((n_heads, nkeys, tm), lambda i: (0, 0, i))
    return pl.pallas_call(
        functools.partial(_pscore_kernel, n_heads=n_heads),
        out_shape=(big, big, big, big),
        grid=(t // tm,),
        in_specs=[pl.BlockSpec((tm, d), lambda i: (i, 0)), _resident(wq.shape), _resident(keys.shape)],
        out_specs=(bspec, bspec, bspec, bspec),
        compiler_params=_params("parallel"),
        name="pscore",
    )(xn, wq, keys)


_GELU_C = 0.7978845608028654
_ACT_ROWS = 32
_GATE_ROWS = 32


def _pdense_kernel(xnt_ref, u_ref, v_ref, c1_ref, s2_ref, w1_ref, w2_ref, x1_ref, g_ref,
                   o_ref, gate_ref, a_ref, act_ref, *, n_heads):
    e = pl.program_id(1)
    n_chunks = pl.num_programs(1) - 1
    nkeys = s2_ref.shape[1]
    ec = u_ref.shape[0]
    tm = s2_ref.shape[2]
    nt = (((1,), (1,)), ((), ()))
    tn = (((0,), (0,)), ((), ()))

    @pl.when(e == 0)
    def _():
        o_ref[...] = jnp.zeros_like(o_ref)
        act_ref[1] = jnp.zeros(act_ref.shape[1:], act_ref.dtype)

    groups = ec // nkeys
    slot = e % 2
    chunk = jnp.minimum(e, n_chunks - 1)
    row0 = (chunk * groups) % SUBLANES

    def gate_groups(g0, g1):
        gis = range(g0, g1)
        w1_rows = {gi: [w1_ref[h, pl.ds(row0 + gi, 1), :] for h in range(n_heads)] for gi in gis}
        c1_rows = {gi: [c1_ref[h, pl.ds(row0 + gi, 1), :] for h in range(n_heads)] for gi in gis}
        for lt in range(tm // LANES):
            ln = slice(lt * LANES, (lt + 1) * LANES)
            for r0 in range(0, nkeys, _GATE_ROWS):
                rs = slice(r0, r0 + _GATE_ROWS)
                accs = {gi: jnp.zeros((_GATE_ROWS, LANES), F32) for gi in gis}
                for h in range(n_heads):
                    s2t = s2_ref[h, rs, ln]
                    w2t = w2_ref[h, rs, ln]
                    for gi in gis:
                        w = w1_rows[gi][h][:, ln] * w2t
                        accs[gi] = accs[gi] + jnp.where(s2t >= c1_rows[gi][h][:, ln], w, 0.0)
                for gi in gis:
                    gate_ref[gi * nkeys + r0:gi * nkeys + r0 + _GATE_ROWS, ln] = accs[gi]

    def act_rows(r0, r1):
        for r in range(r0, r1, _ACT_ROWS):
            rows = slice(r, r + _ACT_ROWS)
            a = a_ref[rows, :]
            th = jnp.tanh(a * (_GELU_C + (_GELU_C * 0.044715) * (a * a)))
            act_ref[slot, rows, :] = ((a + a * th) * gate_ref[rows, :]).astype(BF16)

    halves = 2
    for m in range(halves):
        rows = slice(m * ec // halves, (m + 1) * ec // halves)
        a_ref[rows, :] = jnp.dot(u_ref[rows, :], xnt_ref[...], preferred_element_type=F32)
        gate_groups(m * groups // halves, (m + 1) * groups // halves)
    o_ref[...] += lax.dot_general(act_ref[1 - slot], v_ref[...], tn, preferred_element_type=F32)
    act_rows(0, ec)

    @pl.when(e == n_chunks)
    def _():
        o_ref[...] = _rms_norm(x1_ref[...] + o_ref[...], g_ref[...])


def _pdense(xnt, u, v, c1, s2, w1, w2, x1, g, tm, ec):
    d, t = xnt.shape
    n_chunks = u.shape[0] // ec
    n_heads, nkeys, _ = s2.shape
    groups = ec // nkeys
    assert SUBLANES % groups == 0
    rows4 = pl.BlockSpec((n_heads, SUBLANES, tm),
                         lambda i, e: (0, jnp.minimum(e, n_chunks - 1) * groups // SUBLANES, i))
    once = pl.Buffered(1)
    return pl.pallas_call(
        functools.partial(_pdense_kernel, n_heads=n_heads),
        out_shape=jax.ShapeDtypeStruct((t, d), F32),
        grid=(t // tm, n_chunks + 1),
        in_specs=[
            pl.BlockSpec((d, tm), lambda i, e: (0, i), pipeline_mode=once),
            pl.BlockSpec((ec, d), lambda i, e: (jnp.minimum(e, n_chunks - 1), 0)),
            pl.BlockSpec((ec, d), lambda i, e: (jnp.maximum(e - 1, 0), 0)),
            rows4,
            pl.BlockSpec((n_heads, nkeys, tm), lambda i, e: (0, 0, i)),
            rows4,
            pl.BlockSpec((n_heads, nkeys, tm), lambda i, e: (0, 0, i)),
            pl.BlockSpec((tm, d), lambda i, e: (i, 0), pipeline_mode=once),
            pl.BlockSpec((1, d), lambda i, e: (0, 0)),
        ],
        out_specs=pl.BlockSpec((tm, d), lambda i, e: (i, 0), pipeline_mode=once),
        scratch_shapes=[pltpu.VMEM((ec, tm), F32), pltpu.VMEM((ec, tm), F32),
                        pltpu.VMEM((2, ec, tm), BF16)],
        compiler_params=_params("parallel", "arbitrary", vmem_limit_bytes=PDENSE_VMEM_LIMIT_BYTES),
        name="pdense",
    )(xnt, u, v, c1, s2, w1, w2, x1, g)


def _layer(x, pos, an_g, w_in, gate_bias, conv_w, wa, wb, wo, fn_g, wq, keys, pu, pv, out_g, *,
           batch, seq, tiles):
    t, d = x.shape
    cw = wa.shape[0]
    aw = wb.shape[0]
    n_heads = aw // HEAD_DIM
    p_heads = keys.shape[0]

    inv_freq = 1.0 / (ROPE_THETA ** (jnp.arange(0, ROT_DIM, 2, dtype=F32) / ROT_DIM))
    invf = jnp.concatenate([inv_freq, inv_freq, jnp.zeros((HEAD_DIM - ROT_DIM,), F32)])[None, :]

    proj, q, k, v, kmean = _inproj(x, an_g[None, :], w_in.astype(BF16), pos, invf,
                                   tiles["inproj_tm"], aw, 3 * cw, n_heads)
    kmean = kmean.reshape(batch, seq // MOBA_BLOCK, aw)
    attn = _attn(q, k, v, kmean, batch, seq, n_heads, min(tiles["attn_heads"], n_heads))
    x1, xn, xnt = _merge(proj, attn, x, conv_w, gate_bias[None, :], wa.astype(BF16), wb.astype(BF16),
                    wo.astype(BF16), fn_g[None, :], seq, tiles["merge_tm"])
    keys2 = keys.reshape(p_heads * 2, keys.shape[2], keys.shape[3])
    c1, s2, w1, w2 = _pscore(xn, wq.astype(BF16), keys2, p_heads, tiles["pscore_tm"])
    return _pdense(xnt, pu.astype(BF16), pv.astype(BF16), c1, s2, w1, w2, x1, out_g[None, :],
                   tiles["pdense_tm"], tiles["pdense_ec"])


def _tiles(t, seq):
    return dict(
        inproj_tm=min(1024, t),
        attn_heads=8,
        merge_tm=min(256, seq),
        pscore_tm=min(512, t),
        pdense_tm=min(1024, t), pdense_ec=512,
    )


def kernel(x, positions, attn_norm_g, w_in, gate_bias, conv_w, w_branch_conv, w_branch_attn, w_out,
           ffn_norm_g, w_peer_query, peer_sub_keys, peer_u, peer_v, final_norm_g):
    b, s, d = x.shape
    depth = w_in.shape[0]
    assert depth == 1, "final norm is fused into the last layer's kernel"
    t = b * s
    xt = x.reshape(t, d)
    pos = positions.reshape(t, 1)
    out = _layer(xt, pos, attn_norm_g[0], w_in[0], gate_bias[0], conv_w[0], w_branch_conv[0],
                 w_branch_attn[0], w_out[0], ffn_norm_g[0], w_peer_query[0], peer_sub_keys[0],
                 peer_u[0], peer_v[0], final_norm_g, batch=b, seq=s, tiles=_tiles(t, s))
    return out.reshape(b, s, d)
```

```python
import functools

import jax
import jax.numpy as jnp
from jax import lax
from jax.experimental import pallas as pl
from jax.experimental.pallas import tpu as pltpu

F32 = jnp.float32
BF16 = jnp.bfloat16

HEAD_DIM = 128
ROT_DIM = HEAD_DIM // 4
ROPE_THETA = 500000.0
MOBA_BLOCK = 256
MOBA_TOPK = 3
PEER_TOPK = 16
RMS_EPS = 1e-6

LANES = 128
SUBLANES = 8
VMEM_CAPACITY_BYTES = 64 * 1024 * 1024
VMEM_LIMIT_BYTES = VMEM_CAPACITY_BYTES - 8 * 1024 * 1024
BIG_TILE_VMEM_LIMIT_BYTES = VMEM_CAPACITY_BYTES - 4 * 1024 * 1024

NEG_BIG = -1e30


def _params(*sem, vmem_limit_bytes=VMEM_LIMIT_BYTES):
    return pltpu.CompilerParams(dimension_semantics=sem, vmem_limit_bytes=vmem_limit_bytes)


def _resident(shape):
    zeros = (0,) * len(shape)
    return pl.BlockSpec(shape, lambda *_: zeros, pipeline_mode=pl.Buffered(1))


def _dot_nt_3pass(a, b):
    nt = (((1,), (1,)), ((), ()))
    a_hi, b_hi = a.astype(BF16), b.astype(BF16)
    a_lo = (a - a_hi.astype(F32)).astype(BF16)
    b_lo = (b - b_hi.astype(F32)).astype(BF16)
    dot = lambda x, y: lax.dot_general(x, y, nt, preferred_element_type=F32)
    return dot(a_hi, b_hi) + (dot(a_hi, b_lo) + dot(a_lo, b_hi))


def _rms_norm(x, g):
    ms = jnp.mean(x * x, axis=-1, keepdims=True)
    return x * lax.rsqrt(ms + RMS_EPS) * g


def _inproj_kernel(x_ref, g_ref, w_ref, pos_ref, invf_ref, o_ref, q_ref, k_ref, v_ref, km_ref, h_ref,
                   *, q_tile, n_heads):
    j = pl.program_id(1)

    @pl.when(j == 0)
    def _():
        h_ref[...] = _rms_norm(x_ref[...], g_ref[...]).astype(BF16)

    o_ref[...] = jnp.dot(h_ref[...], w_ref[...], preferred_element_type=F32)

    def rotary():
        half = ROT_DIM // 2
        ang = pos_ref[...].astype(F32) * invf_ref[...]
        cos = jnp.cos(ang)
        sin = jnp.sin(ang)
        lane = lax.broadcasted_iota(jnp.int32, ang.shape, 1)
        sin_lo = jnp.where(lane < half, -sin, 0.0)
        sin_hi = jnp.where((lane >= half) & (lane < ROT_DIM), sin, 0.0)
        return lambda x: (x * cos + pltpu.roll(x, HEAD_DIM - half, 1) * sin_lo
                          + pltpu.roll(x, half, 1) * sin_hi)

    heads = [slice(h * HEAD_DIM, (h + 1) * HEAD_DIM) for h in range(n_heads)]

    @pl.when(j == q_tile)
    def _():
        rot = rotary()
        for hs in heads:
            q_ref[:, hs] = rot(o_ref[:, hs])

    @pl.when(j == q_tile + 1)
    def _():
        rot = rotary()
        for hs in heads:
            kr = rot(o_ref[:, hs])
            k_ref[:, hs] = kr.astype(BF16)
            for b in range(km_ref.shape[0]):
                km_ref[b, :, hs] = jnp.mean(kr[b * MOBA_BLOCK:(b + 1) * MOBA_BLOCK, :], axis=0, keepdims=True)

    @pl.when(j == q_tile + 2)
    def _():
        v_ref[...] = o_ref[...].astype(BF16)


def _inproj(x, g, w, pos, invf, tm, tn, q_col, n_heads):
    t, d = x.shape
    n = w.shape[1]
    aw = n_heads * HEAD_DIM
    assert tn == aw and q_col % tn == 0 and tm % MOBA_BLOCK == 0
    row = lambda i, j: (i, 0)
    return pl.pallas_call(
        functools.partial(_inproj_kernel, q_tile=q_col // tn, n_heads=n_heads),
        out_shape=(
            jax.ShapeDtypeStruct((t, n), F32),
            jax.ShapeDtypeStruct((t, aw), F32),
            jax.ShapeDtypeStruct((t, aw), BF16),
            jax.ShapeDtypeStruct((t, aw), BF16),
            jax.ShapeDtypeStruct((t // MOBA_BLOCK, 1, aw), F32),
        ),
        grid=(t // tm, n // tn),
        in_specs=[
            pl.BlockSpec((tm, d), row),
            pl.BlockSpec((1, d), lambda i, j: (0, 0)),
            pl.BlockSpec((d, tn), lambda i, j: (0, j)),
            pl.BlockSpec((tm, 1), row),
            pl.BlockSpec((1, HEAD_DIM), lambda i, j: (0, 0)),
        ],
        out_specs=(
            pl.BlockSpec((tm, tn), lambda i, j: (i, j)),
            pl.BlockSpec((tm, aw), row),
            pl.BlockSpec((tm, aw), row),
            pl.BlockSpec((tm, aw), row),
            pl.BlockSpec((tm // MOBA_BLOCK, 1, aw), lambda i, j: (i, 0, 0)),
        ),
        scratch_shapes=[pltpu.VMEM((tm, d), BF16)],
        compiler_params=_params("parallel", "arbitrary", vmem_limit_bytes=BIG_TILE_VMEM_LIMIT_BYTES),
        name="inproj",
    )(x, g, w, pos, invf)


_LOG2E = 1.4426950408889634


def _attn_kernel(q_ref, k_ref, v_ref, km_ref, o_ref, *, scale, heads):
    i = pl.program_id(2)
    blk = MOBA_BLOCK
    span = 2 * blk
    hd = HEAD_DIM
    nb = km_ref.shape[1]
    c = scale * _LOG2E
    nt = (((1,), (1,)), ((), ()))
    pair = lax.shift_right_logical(i, 1)
    odd = i - 2 * pair
    ones = jnp.ones((span, LANES), BF16)
    lane = lax.broadcasted_iota(jnp.int32, (span, LANES), 1)
    upper = jnp.where(lax.broadcasted_iota(jnp.int32, (span, LANES), 0) >= blk, 1, 0)
    row = lax.broadcasted_iota(jnp.int32, (blk, span), 0)
    colv = lax.broadcasted_iota(jnp.int32, (blk, span), 1)
    bidx = lax.broadcasted_iota(jnp.int32, (nb, blk), 0)

    def kv_pair(p, hs):
        start = pl.multiple_of(p * span, span)
        onehot = jnp.where(lane == 2 * p + upper, 1.0, 0.0).astype(BF16)
        k_aug = jnp.concatenate([k_ref[pl.ds(start, span), hs], onehot], axis=1)
        v_aug = jnp.concatenate([v_ref[pl.ds(start, span), hs], ones], axis=1)
        return k_aug, v_aug

    q_augs = []
    init = []
    for h in range(heads):
        hs = slice(h * hd, (h + 1) * hd)
        q = q_ref[:, hs]
        gate = _dot_nt_3pass(km_ref[0, :, hs], q)
        gate = jnp.where(bidx < i, gate, -jnp.inf)
        sel = jnp.zeros(gate.shape, F32)
        for r in range(MOBA_TOPK):
            top = jnp.max(gate, axis=0, keepdims=True)
            first = jnp.min(jnp.where(gate == top, bidx, nb), axis=0, keepdims=True)
            hit = bidx == first
            sel = jnp.where(hit & (r < i), 1.0, sel)
            gate = jnp.where(hit, -jnp.inf, gate)
        bias_t = jnp.where((sel > 0.0) | (bidx == i), 0.0, NEG_BIG)
        bias_t = jnp.concatenate([bias_t, jnp.zeros((LANES - nb, blk), F32)], axis=0)
        bias = bias_t.T.astype(BF16)
        q_aug = jnp.concatenate([q.astype(BF16), bias], axis=1)
        q_augs.append(q_aug)

        k_aug, v_aug = kv_pair(pair, hs)
        s = lax.dot_general(q_aug, k_aug, nt, preferred_element_type=F32) * c
        s = jnp.where(colv - row > odd * blk, -jnp.inf, s)
        m0 = jnp.max(s, axis=-1, keepdims=True)
        p = jnp.exp2(s - m0).astype(BF16)
        init.append((m0, jnp.dot(p, v_aug, preferred_element_type=F32)))

    def body(j, carry):
        out = []
        for h in range(heads):
            hs = slice(h * hd, (h + 1) * hd)
            m, acc = carry[h]
            k_aug, v_aug = kv_pair(j, hs)
            s = lax.dot_general(q_augs[h], k_aug, nt, preferred_element_type=F32) * c
            m_new = jnp.maximum(m, jnp.max(s, axis=-1, keepdims=True))
            p = jnp.exp2(s - m_new).astype(BF16)
            acc = jnp.exp2(m - m_new) * acc + jnp.dot(p, v_aug, preferred_element_type=F32)
            out.append((m_new, acc))
        return tuple(out)

    final = lax.fori_loop(0, pair, body, tuple(init))
    for h in range(heads):
        acc = final[h][1]
        o_ref[:, h * hd:(h + 1) * hd] = (acc[:, :hd] / acc[:, hd:]).astype(o_ref.dtype)


def _attn(q, k, v, kmean, batch, seq, n_heads, heads):
    t, w = q.shape
    nq = seq // MOBA_BLOCK
    blk = MOBA_BLOCK
    hw = heads * HEAD_DIM
    assert nq % 2 == 0 and nq <= LANES, "key blocks are visited in pairs; mask columns fit one lane tile"
    return pl.pallas_call(
        functools.partial(_attn_kernel, scale=HEAD_DIM ** -0.5, heads=heads),
        out_shape=jax.ShapeDtypeStruct((t, w), BF16),
        grid=(batch, n_heads // heads, nq),
        in_specs=[
            pl.BlockSpec((blk, hw), lambda b, g, i: (b * nq + i, g)),
            pl.BlockSpec((seq, hw), lambda b, g, i: (b, g)),
            pl.BlockSpec((seq, hw), lambda b, g, i: (b, g)),
            pl.BlockSpec((1, nq, hw), lambda b, g, i: (b, 0, g)),
        ],
        out_specs=pl.BlockSpec((blk, hw), lambda b, g, i: (b * nq + i, g)),
        compiler_params=_params("parallel", "parallel", "arbitrary"),
        name="attn",
    )(q, k, v, kmean)


def _merge_kernel(cb_ref, cc_ref, cx_ref, pc_ref, px_ref, ga_ref, gb_ref, at_ref, x_ref,
                  cw_ref, bias_ref, wa_ref, wb_ref, wo_ref, ng_ref,
                  x1_ref, xn_ref, xnt_ref, z_ref, *, seq):
    tm = cb_ref.shape[0]
    d = x_ref.shape[1]
    first = (pl.program_id(0) * tm) % seq == 0
    z = cc_ref[...] * cx_ref[...]
    z_prev = jnp.where(first, 0.0, pc_ref[...] * px_ref[...])
    z_ref[0:SUBLANES, :] = z_prev
    z_ref[SUBLANES:, :] = z
    cw = cw_ref[...]
    conv = (cw[2:3, :] * z
            + cw[1:2, :] * z_ref[SUBLANES - 1:SUBLANES - 1 + tm, :]
            + cw[0:1, :] * z_ref[SUBLANES - 2:SUBLANES - 2 + tm, :])
    u = (cb_ref[...] * conv).astype(BF16)
    y_conv = jnp.dot(u, wa_ref[...], preferred_element_type=F32)
    y_attn = jnp.dot(at_ref[...], wb_ref[...], preferred_element_type=F32)
    bias = bias_ref[...]
    merged = (jax.nn.sigmoid(ga_ref[...] + bias[:, :d]) * y_conv
              + jax.nn.sigmoid(gb_ref[...] + bias[:, d:]) * y_attn)
    x1 = x_ref[...] + jnp.dot(merged.astype(BF16), wo_ref[...], preferred_element_type=F32)
    x1_ref[...] = x1
    xn = _rms_norm(x1, ng_ref[...])
    xn_ref[...] = xn.astype(BF16)
    xnt_ref[...] = xn.T.astype(BF16)


def _merge(proj, attn, x, conv_w, gate_bias, wa, wb, wo, ng, seq, tm):
    t, d = x.shape
    cw = wa.shape[0]
    aw = wb.shape[0]
    gcol = (3 * cw + 3 * aw) // d
    halo = lambda c: pl.BlockSpec(
        (SUBLANES, cw), lambda i: (jnp.maximum(i * (tm // SUBLANES) - 1, 0), c))
    return pl.pallas_call(
        functools.partial(_merge_kernel, seq=seq),
        out_shape=(jax.ShapeDtypeStruct((t, d), F32), jax.ShapeDtypeStruct((t, d), BF16),
                   jax.ShapeDtypeStruct((d, t), BF16)),
        grid=(t // tm,),
        in_specs=[
            pl.BlockSpec((tm, cw), lambda i: (i, 0)),
            pl.BlockSpec((tm, cw), lambda i: (i, 1)),
            pl.BlockSpec((tm, cw), lambda i: (i, 2)),
            halo(1), halo(2),
            pl.BlockSpec((tm, d), lambda i: (i, gcol)),
            pl.BlockSpec((tm, d), lambda i: (i, gcol + 1)),
            pl.BlockSpec((tm, aw), lambda i: (i, 0)),
            pl.BlockSpec((tm, d), lambda i: (i, 0)),
            _resident(conv_w.shape), _resident(gate_bias.shape),
            _resident(wa.shape), _resident(wb.shape), _resident(wo.shape), _resident(ng.shape),
        ],
        out_specs=(pl.BlockSpec((tm, d), lambda i: (i, 0)), pl.BlockSpec((tm, d), lambda i: (i, 0)),
                   pl.BlockSpec((d, tm), lambda i: (0, i))),
        scratch_shapes=[pltpu.VMEM((tm + SUBLANES, cw), F32)],
        compiler_params=_params("parallel"),
        name="merge",
    )(proj, proj, proj, proj, proj, proj, proj, attn, x, conv_w, gate_bias, wa, wb, wo, ng)


def _sort_network(n):
    pairs = []

    def merge(lo, hi, r):
        step = r * 2
        if step < hi - lo:
            merge(lo, hi, step)
            merge(lo + r, hi, step)
            pairs.extend((i, i + r) for i in range(lo + r, hi - r, step))
        else:
            pairs.append((lo, lo + r))

    def sort(lo, hi):
        if hi - lo >= 1:
            mid = lo + (hi - lo) // 2
            sort(lo, mid)
            sort(mid + 1, hi)
            merge(lo, hi, 1)

    sort(0, n - 1)
    return pairs


def _compare_exchange(xs, i, j):
    a, b = xs[i], xs[j]
    if b is None:
        return
    if a is None:
        xs[i], xs[j] = b, None
        return
    xs[i], xs[j] = jnp.maximum(a, b), jnp.minimum(a, b)


def _sort_desc(xs):
    n = pl.next_power_of_2(len(xs))
    xs = list(xs) + [None] * (n - len(xs))
    for i, j in _sort_network(n):
        _compare_exchange(xs, i, j)
    return xs


def _top_per_column(s):
    n = s.shape[0] // SUBLANES
    xs = _sort_desc([s[SUBLANES * k:SUBLANES * (k + 1), :] for k in range(n)])
    r = SUBLANES // 2
    while r >= 1:
        xs = [jnp.maximum(xs[k], pltpu.roll(xs[n - 1 - k], SUBLANES - r, 0)) for k in range(n)]
        d = n // 2
        while d >= 1:
            for k in range(n):
                if k & d == 0:
                    _compare_exchange(xs, k, k + d)
            d //= 2
        r //= 2
    return xs


def _pscore_kernel(xn_ref, wq_ref, keys_ref, c1_ref, s2_ref, w1_ref, w2_ref, *, n_heads):
    nkeys, half = keys_ref.shape[1], keys_ref.shape[2]
    tm = xn_ref.shape[0]
    topk = PEER_TOPK
    qp = jnp.dot(xn_ref[...], wq_ref[...], preferred_element_type=F32)
    nt = (((1,), (1,)), ((), ()))
    sub = lax.broadcasted_iota(jnp.int32, (SUBLANES, tm), 0)

    tops = [[jnp.zeros((SUBLANES, tm), F32)] * (topk + 1) for _ in range(2)]
    for h in range(n_heads):
        for p, ref in enumerate((c1_ref, s2_ref)):
            c = (2 * h + p) * half
            s = _dot_nt_3pass(keys_ref[2 * h + p], qp[:, c:c + half])
            ref[h] = s
            col = _top_per_column(s)
            below = jnp.max(jnp.where(s < col[topk - 1][0:1, :], s, -jnp.inf), axis=0, keepdims=True)
            col = [pltpu.roll(x, h, 0) if h else x for x in col] + [below]
            tops[p] = [jnp.where(sub == h, col[k], tops[p][k]) for k in range(topk + 1)]
    a, b = tops

    cand = [a[p] + b[q] for p in range(topk + 1) for q in range(topk + 1)
            if (p + 1) * (q + 1) <= topk + 1]
    best = _sort_desc(cand)[:topk + 1]
    z = best[0] * 0.0
    for k in range(topk):
        z = z + jnp.exp(best[k] - best[0])
    tau = 0.5 * (best[topk - 1] + best[topk])
    half_inv_z = 0.5 / z

    for h in range(n_heads):
        row = slice(h, h + 1)
        s1 = c1_ref[h]
        w1_ref[h] = jnp.exp(s1 - a[0][row, :]) * half_inv_z[row, :]
        w2_ref[h] = jnp.exp(s2_ref[h] - b[0][row, :])
        c1_ref[h] = tau[row, :] - s1


def _pscore(xn, wq, keys, n_heads, tm):
    t, d = xn.shape
    nkeys = keys.shape[1]
    assert n_heads == SUBLANES and nkeys == SUBLANES * PEER_TOPK
    big = jax.ShapeDtypeStruct((n_heads, nkeys, t), F32)
    bspec = pl.BlockSpec((n_heads, nkeys, tm), lambda i: (0, 0, i))
    return pl.pallas_call(
        functools.partial(_pscore_kernel, n_heads=n_heads),
        out_shape=(big, big, big, big),
        grid=(t // tm,),
        in_specs=[pl.BlockSpec((tm, d), lambda i: (i, 0)), _resident(wq.shape), _resident(keys.shape)],
        out_specs=(bspec, bspec, bspec, bspec),
        compiler_params=_params("parallel"),
        name="pscore",
    )(xn, wq, keys)


_GELU_C = 0.7978845608028654
_ACT_ROWS = 32
_GATE_ROWS = 32


def _pdense_kernel(xnt_ref, u_ref, v_ref, c1_ref, s2_ref, w1_ref, w2_ref, x1_ref, g_ref,
                   o_ref, gate_ref, a_ref, act_ref, *, n_heads):
    e = pl.program_id(1)
    n_chunks = pl.num_programs(1) - 1
    nkeys = s2_ref.shape[1]
    ec = u_ref.shape[0]
    tm = s2_ref.shape[2]
    tn = (((0,), (0,)), ((), ()))

    @pl.when(e == 0)
    def _():
        o_ref[...] = jnp.zeros_like(o_ref)
        act_ref[1] = jnp.zeros(act_ref.shape[1:], act_ref.dtype)

    groups = ec // nkeys
    slot = e % 2
    chunk = jnp.minimum(e, n_chunks - 1)
    row0 = (chunk * groups) % SUBLANES

    def gate_groups(g0, g1):
        gis = range(g0, g1)
        w1_rows = {gi: [w1_ref[h, pl.ds(row0 + gi, 1), :] for h in range(n_heads)] for gi in gis}
        c1_rows = {gi: [c1_ref[h, pl.ds(row0 + gi, 1), :] for h in range(n_heads)] for gi in gis}
        for lt in range(tm // LANES):
            ln = slice(lt * LANES, (lt + 1) * LANES)
            for r0 in range(0, nkeys, _GATE_ROWS):
                rs = slice(r0, r0 + _GATE_ROWS)
                accs = {gi: jnp.zeros((_GATE_ROWS, LANES), F32) for gi in gis}
                for h in range(n_heads):
                    s2t = s2_ref[h, rs, ln]
                    w2t = w2_ref[h, rs, ln]
                    for gi in gis:
                        w = w1_rows[gi][h][:, ln] * w2t
                        accs[gi] = accs[gi] + jnp.where(s2t >= c1_rows[gi][h][:, ln], w, 0.0)
                for gi in gis:
                    gate_ref[gi * nkeys + r0:gi * nkeys + r0 + _GATE_ROWS, ln] = accs[gi]

    def act_rows(r0, r1):
        for r in range(r0, r1, _ACT_ROWS):
            rows = slice(r, r + _ACT_ROWS)
            a = a_ref[rows, :]
            th = jnp.tanh(a * (_GELU_C + (_GELU_C * 0.044715) * (a * a)))
            act_ref[slot, rows, :] = ((a + a * th) * gate_ref[rows, :]).astype(BF16)

    halves = 2
    for m in range(halves):
        rows = slice(m * ec // halves, (m + 1) * ec // halves)
        a_ref[rows, :] = jnp.dot(u_ref[rows, :], xnt_ref[...], preferred_element_type=F32)
        gate_groups(m * groups // halves, (m + 1) * groups // halves)
    o_ref[...] += lax.dot_general(act_ref[1 - slot], v_ref[...], tn, preferred_element_type=F32)
    act_rows(0, ec)

    @pl.when(e == n_chunks)
    def _():
        o_ref[...] = _rms_norm(x1_ref[...] + o_ref[...], g_ref[...])


def _pdense(xnt, u, v, c1, s2, w1, w2, x1, g, tm, ec):
    d, t = xnt.shape
    n_chunks = u.shape[0] // ec
    n_heads, nkeys, _ = s2.shape
    groups = ec // nkeys
    assert SUBLANES % groups == 0
    rows4 = pl.BlockSpec((n_heads, SUBLANES, tm),
                         lambda i, e: (0, jnp.minimum(e, n_chunks - 1) * groups // SUBLANES, i))
    once = pl.Buffered(1)
    return pl.pallas_call(
        functools.partial(_pdense_kernel, n_heads=n_heads),
        out_shape=jax.ShapeDtypeStruct((t, d), F32),
        grid=(t // tm, n_chunks + 1),
        in_specs=[
            pl.BlockSpec((d, tm), lambda i, e: (0, i), pipeline_mode=once),
            pl.BlockSpec((ec, d), lambda i, e: (jnp.minimum(e, n_chunks - 1), 0)),
            pl.BlockSpec((ec, d), lambda i, e: (jnp.maximum(e - 1, 0), 0)),
            rows4,
            pl.BlockSpec((n_heads, nkeys, tm), lambda i, e: (0, 0, i)),
            rows4,
            pl.BlockSpec((n_heads, nkeys, tm), lambda i, e: (0, 0, i)),
            pl.BlockSpec((tm, d), lambda i, e: (i, 0), pipeline_mode=once),
            pl.BlockSpec((1, d), lambda i, e: (0, 0)),
        ],
        out_specs=pl.BlockSpec((tm, d), lambda i, e: (i, 0), pipeline_mode=once),
        scratch_shapes=[pltpu.VMEM((ec, tm), F32), pltpu.VMEM((ec, tm), F32),
                        pltpu.VMEM((2, ec, tm), BF16)],
        compiler_params=_params("parallel", "arbitrary", vmem_limit_bytes=BIG_TILE_VMEM_LIMIT_BYTES),
        name="pdense",
    )(xnt, u, v, c1, s2, w1, w2, x1, g)


def _layer(x, pos, an_g, w_in, gate_bias, conv_w, wa, wb, wo, fn_g, wq, keys, pu, pv, out_g, *,
           batch, seq, tiles):
    cw = wa.shape[0]
    aw = wb.shape[0]
    n_heads = aw // HEAD_DIM
    p_heads = keys.shape[0]

    inv_freq = 1.0 / (ROPE_THETA ** (jnp.arange(0, ROT_DIM, 2, dtype=F32) / ROT_DIM))
    invf = jnp.concatenate([inv_freq, inv_freq, jnp.zeros((HEAD_DIM - ROT_DIM,), F32)])[None, :]

    proj, q, k, v, kmean = _inproj(x, an_g[None, :], w_in.astype(BF16), pos, invf,
                                   tiles["inproj_tm"], aw, 3 * cw, n_heads)
    kmean = kmean.reshape(batch, seq // MOBA_BLOCK, aw)
    attn = _attn(q, k, v, kmean, batch, seq, n_heads, min(tiles["attn_heads"], n_heads))
    x1, xn, xnt = _merge(proj, attn, x, conv_w, gate_bias[None, :], wa.astype(BF16), wb.astype(BF16),
                    wo.astype(BF16), fn_g[None, :], seq, tiles["merge_tm"])
    keys2 = keys.reshape(p_heads * 2, keys.shape[2], keys.shape[3])
    c1, s2, w1, w2 = _pscore(xn, wq.astype(BF16), keys2, p_heads, tiles["pscore_tm"])
    return _pdense(xnt, pu.astype(BF16), pv.astype(BF16), c1, s2, w1, w2, x1, out_g[None, :],
                   tiles["pdense_tm"], tiles["pdense_ec"])


def _tiles(t, seq):
    return dict(
        inproj_tm=min(1024, t),
        attn_heads=8,
        merge_tm=min(256, seq),
        pscore_tm=min(512, t),
        pdense_tm=min(1024, t), pdense_ec=512,
    )


def kernel(x, positions, attn_norm_g, w_in, gate_bias, conv_w, w_branch_conv, w_branch_attn, w_out,
           ffn_norm_g, w_peer_query, peer_sub_keys, peer_u, peer_v, final_norm_g):
    b, s, d = x.shape
    depth = w_in.shape[0]
    assert depth == 1, "final norm is fused into the last layer's kernel"
    t = b * s
    xt = x.reshape(t, d)
    pos = positions.reshape(t, 1)
    out = _layer(xt, pos, attn_norm_g[0], w_in[0], gate_bias[0], conv_w[0], w_branch_conv[0],
                 w_branch_attn[0], w_out[0], ffn_norm_g[0], w_peer_query[0], peer_sub_keys[0],
                 peer_u[0], peer_v[0], final_norm_g, batch=b, seq=s, tiles=_tiles(t, s))
    return out.reshape(b, s, d)
```

```python
import functools

import jax
import jax.numpy as jnp
from jax import lax
from jax.experimental import pallas as pl
from jax.experimental.pallas import tpu as pltpu

F32 = jnp.float32
BF16 = jnp.bfloat16

HEAD_DIM = 128
ROT_DIM = HEAD_DIM // 4
ROPE_THETA = 500000.0
MOBA_BLOCK = 256
MOBA_TOPK = 3
PEER_TOPK = 16
RMS_EPS = 1e-6

LANES = 128
SUBLANES = 8
VMEM_CAPACITY_BYTES = 64 * 1024 * 1024
VMEM_LIMIT_BYTES = VMEM_CAPACITY_BYTES - 8 * 1024 * 1024
BIG_TILE_VMEM_LIMIT_BYTES = VMEM_CAPACITY_BYTES - 4 * 1024 * 1024

NEG_BIG = -1e30


def _params(*sem, vmem_limit_bytes=VMEM_LIMIT_BYTES):
    return pltpu.CompilerParams(dimension_semantics=sem, vmem_limit_bytes=vmem_limit_bytes)


def _resident(shape):
    zeros = (0,) * len(shape)
    return pl.BlockSpec(shape, lambda *_: zeros, pipeline_mode=pl.Buffered(1))


def _dot_nt_3pass(a, b):
    nt = (((1,), (1,)), ((), ()))
    a_hi, b_hi = a.astype(BF16), b.astype(BF16)
    a_lo = (a - a_hi.astype(F32)).astype(BF16)
    b_lo = (b - b_hi.astype(F32)).astype(BF16)
    dot = lambda x, y: lax.dot_general(x, y, nt, preferred_element_type=F32)
    return dot(a_hi, b_hi) + (dot(a_hi, b_lo) + dot(a_lo, b_hi))


def _rms_norm(x, g):
    ms = jnp.mean(x * x, axis=-1, keepdims=True)
    return x * lax.rsqrt(ms + RMS_EPS) * g


def _inproj_kernel(x_ref, g_ref, w_ref, pos_ref, invf_ref, o_ref, q_ref, k_ref, v_ref, km_ref, h_ref,
                   *, q_tile, n_heads):
    j = pl.program_id(1)

    @pl.when(j == 0)
    def _():
        h_ref[...] = _rms_norm(x_ref[...], g_ref[...]).astype(BF16)

    o_ref[...] = jnp.dot(h_ref[...], w_ref[...], preferred_element_type=F32)

    def rotary():
        half = ROT_DIM // 2
        ang = pos_ref[...].astype(F32) * invf_ref[...]
        cos = jnp.cos(ang)
        sin = jnp.sin(ang)
        lane = lax.broadcasted_iota(jnp.int32, ang.shape, 1)
        sin_lo = jnp.where(lane < half, -sin, 0.0)
        sin_hi = jnp.where((lane >= half) & (lane < ROT_DIM), sin, 0.0)
        return lambda x: (x * cos + pltpu.roll(x, HEAD_DIM - half, 1) * sin_lo
                          + pltpu.roll(x, half, 1) * sin_hi)

    heads = [slice(h * HEAD_DIM, (h + 1) * HEAD_DIM) for h in range(n_heads)]

    @pl.when(j == q_tile)
    def _():
        rot = rotary()
        for hs in heads:
            q_ref[:, hs] = rot(o_ref[:, hs])

    @pl.when(j == q_tile + 1)
    def _():
        rot = rotary()
        for hs in heads:
            kr = rot(o_ref[:, hs])
            k_ref[:, hs] = kr.astype(BF16)
            for b in range(km_ref.shape[0]):
                km_ref[b, :, hs] = jnp.mean(kr[b * MOBA_BLOCK:(b + 1) * MOBA_BLOCK, :], axis=0, keepdims=True)

    @pl.when(j == q_tile + 2)
    def _():
        v_ref[...] = o_ref[...].astype(BF16)


def _inproj(x, g, w, pos, invf, tm, tn, q_col, n_heads):
    t, d = x.shape
    n = w.shape[1]
    aw = n_heads * HEAD_DIM
    assert tn == aw and q_col % tn == 0 and tm % MOBA_BLOCK == 0
    row = lambda i, j: (i, 0)
    return pl.pallas_call(
        functools.partial(_inproj_kernel, q_tile=q_col // tn, n_heads=n_heads),
        out_shape=(
            jax.ShapeDtypeStruct((t, n), F32),
            jax.ShapeDtypeStruct((t, aw), F32),
            jax.ShapeDtypeStruct((t, aw), BF16),
            jax.ShapeDtypeStruct((t, aw), BF16),
            jax.ShapeDtypeStruct((t // MOBA_BLOCK, 1, aw), F32),
        ),
        grid=(t // tm, n // tn),
        in_specs=[
            pl.BlockSpec((tm, d), row),
            pl.BlockSpec((1, d), lambda i, j: (0, 0)),
            pl.BlockSpec((d, tn), lambda i, j: (0, j)),
            pl.BlockSpec((tm, 1), row),
            pl.BlockSpec((1, HEAD_DIM), lambda i, j: (0, 0)),
        ],
        out_specs=(
            pl.BlockSpec((tm, tn), lambda i, j: (i, j)),
            pl.BlockSpec((tm, aw), row),
            pl.BlockSpec((tm, aw), row),
            pl.BlockSpec((tm, aw), row),
            pl.BlockSpec((tm // MOBA_BLOCK, 1, aw), lambda i, j: (i, 0, 0)),
        ),
        scratch_shapes=[pltpu.VMEM((tm, d), BF16)],
        compiler_params=_params("parallel", "arbitrary", vmem_limit_bytes=BIG_TILE_VMEM_LIMIT_BYTES),
        name="inproj",
    )(x, g, w, pos, invf)


_LOG2E = 1.4426950408889634


def _attn_kernel(q_ref, k_ref, v_ref, km_ref, o_ref, *, scale, heads):
    i = pl.program_id(2)
    blk = MOBA_BLOCK
    span = 2 * blk
    hd = HEAD_DIM
    nb = km_ref.shape[1]
    c = scale * _LOG2E
    nt = (((1,), (1,)), ((), ()))
    pair = lax.shift_right_logical(i, 1)
    odd = i - 2 * pair
    ones = jnp.ones((span, LANES), BF16)
    lane = lax.broadcasted_iota(jnp.int32, (span, LANES), 1)
    upper = jnp.where(lax.broadcasted_iota(jnp.int32, (span, LANES), 0) >= blk, 1, 0)
    row = lax.broadcasted_iota(jnp.int32, (blk, span), 0)
    colv = lax.broadcasted_iota(jnp.int32, (blk, span), 1)
    bidx = lax.broadcasted_iota(jnp.int32, (nb, blk), 0)

    def kv_pair(p, hs):
        start = pl.multiple_of(p * span, span)
        onehot = jnp.where(lane == 2 * p + upper, 1.0, 0.0).astype(BF16)
        k_aug = jnp.concatenate([k_ref[pl.ds(start, span), hs], onehot], axis=1)
        v_aug = jnp.concatenate([v_ref[pl.ds(start, span), hs], ones], axis=1)
        return k_aug, v_aug

    q_augs = []
    init = []
    for h in range(heads):
        hs = slice(h * hd, (h + 1) * hd)
        q = q_ref[:, hs]
        gate = _dot_nt_3pass(km_ref[0, :, hs], q)
        gate = jnp.where(bidx < i, gate, -jnp.inf)
        sel = jnp.zeros(gate.shape, F32)
        for r in range(MOBA_TOPK):
            top = jnp.max(gate, axis=0, keepdims=True)
            first = jnp.min(jnp.where(gate == top, bidx, nb), axis=0, keepdims=True)
            hit = bidx == first
            sel = jnp.where(hit & (r < i), 1.0, sel)
            gate = jnp.where(hit, -jnp.inf, gate)
        bias_t = jnp.where((sel > 0.0) | (bidx == i), 0.0, NEG_BIG)
        bias_t = jnp.concatenate([bias_t, jnp.zeros((LANES - nb, blk), F32)], axis=0)
        bias = bias_t.T.astype(BF16)
        q_aug = jnp.concatenate([q.astype(BF16), bias], axis=1)
        q_augs.append(q_aug)

        k_aug, v_aug = kv_pair(pair, hs)
        s = lax.dot_general(q_aug, k_aug, nt, preferred_element_type=F32) * c
        s = jnp.where(colv - row > odd * blk, -jnp.inf, s)
        m0 = jnp.max(s, axis=-1, keepdims=True)
        p = jnp.exp2(s - m0).astype(BF16)
        init.append((m0, jnp.dot(p, v_aug, preferred_element_type=F32)))

    def body(j, carry):
        out = []
        for h in range(heads):
            hs = slice(h * hd, (h + 1) * hd)
            m, acc = carry[h]
            k_aug, v_aug = kv_pair(j, hs)
            s = lax.dot_general(q_augs[h], k_aug, nt, preferred_element_type=F32) * c
            m_new = jnp.maximum(m, jnp.max(s, axis=-1, keepdims=True))
            p = jnp.exp2(s - m_new).astype(BF16)
            acc = jnp.exp2(m - m_new) * acc + jnp.dot(p, v_aug, preferred_element_type=F32)
            out.append((m_new, acc))
        return tuple(out)

    final = lax.fori_loop(0, pair, body, tuple(init))
    for h in range(heads):
        acc = final[h][1]
        o_ref[:, h * hd:(h + 1) * hd] = (acc[:, :hd] / acc[:, hd:]).astype(o_ref.dtype)


def _attn(q, k, v, kmean, batch, seq, n_heads, heads):
    t, w = q.shape
    nq = seq // MOBA_BLOCK
    blk = MOBA_BLOCK
    hw = heads * HEAD_DIM
    assert nq % 2 == 0 and nq <= LANES, "key blocks are visited in pairs; mask columns fit one lane tile"
    return pl.pallas_call(
        functools.partial(_attn_kernel, scale=HEAD_DIM ** -0.5, heads=heads),
        out_shape=jax.ShapeDtypeStruct((t, w), BF16),
        grid=(batch, n_heads // heads, nq),
        in_specs=[
            pl.BlockSpec((blk, hw), lambda b, g, i: (b * nq + i, g)),
            pl.BlockSpec((seq, hw), lambda b, g, i: (b, g)),
            pl.BlockSpec((seq, hw), lambda b, g, i: (b, g)),
            pl.BlockSpec((1, nq, hw), lambda b, g, i: (b, 0, g)),
        ],
        out_specs=pl.BlockSpec((blk, hw), lambda b, g, i: (b * nq + i, g)),
        compiler_params=_params("parallel", "parallel", "arbitrary"),
        name="attn",
    )(q, k, v, kmean)


def _merge_kernel(cb_ref, cc_ref, cx_ref, pc_ref, px_ref, ga_ref, gb_ref, at_ref, x_ref,
                  cw_ref, bias_ref, wa_ref, wb_ref, wo_ref, ng_ref,
                  x1_ref, xn_ref, xnt_ref, z_ref, *, seq):
    tm = cb_ref.shape[0]
    d = x_ref.shape[1]
    first = (pl.program_id(0) * tm) % seq == 0
    z = cc_ref[...] * cx_ref[...]
    z_prev = jnp.where(first, 0.0, pc_ref[...] * px_ref[...])
    z_ref[0:SUBLANES, :] = z_prev
    z_ref[SUBLANES:, :] = z
    cw = cw_ref[...]
    conv = (cw[2:3, :] * z
            + cw[1:2, :] * z_ref[SUBLANES - 1:SUBLANES - 1 + tm, :]
            + cw[0:1, :] * z_ref[SUBLANES - 2:SUBLANES - 2 + tm, :])
    u = (cb_ref[...] * conv).astype(BF16)
    y_conv = jnp.dot(u, wa_ref[...], preferred_element_type=F32)
    y_attn = jnp.dot(at_ref[...], wb_ref[...], preferred_element_type=F32)
    bias = bias_ref[...]
    merged = (jax.nn.sigmoid(ga_ref[...] + bias[:, :d]) * y_conv
              + jax.nn.sigmoid(gb_ref[...] + bias[:, d:]) * y_attn)
    x1 = x_ref[...] + jnp.dot(merged.astype(BF16), wo_ref[...], preferred_element_type=F32)
    x1_ref[...] = x1
    xn = _rms_norm(x1, ng_ref[...])
    xn_ref[...] = xn.astype(BF16)
    xnt_ref[...] = xn.T.astype(BF16)


def _merge(proj, attn, x, conv_w, gate_bias, wa, wb, wo, ng, seq, tm):
    t, d = x.shape
    cw = wa.shape[0]
    aw = wb.shape[0]
    gcol = (3 * cw + 3 * aw) // d
    halo = lambda c: pl.BlockSpec(
        (SUBLANES, cw), lambda i: (jnp.maximum(i * (tm // SUBLANES) - 1, 0), c))
    return pl.pallas_call(
        functools.partial(_merge_kernel, seq=seq),
        out_shape=(jax.ShapeDtypeStruct((t, d), F32), jax.ShapeDtypeStruct((t, d), BF16),
                   jax.ShapeDtypeStruct((d, t), BF16)),
        grid=(t // tm,),
        in_specs=[
            pl.BlockSpec((tm, cw), lambda i: (i, 0)),
            pl.BlockSpec((tm, cw), lambda i: (i, 1)),
            pl.BlockSpec((tm, cw), lambda i: (i, 2)),
            halo(1), halo(2),
            pl.BlockSpec((tm, d), lambda i: (i, gcol)),
            pl.BlockSpec((tm, d), lambda i: (i, gcol + 1)),
            pl.BlockSpec((tm, aw), lambda i: (i, 0)),
            pl.BlockSpec((tm, d), lambda i: (i, 0)),
            _resident(conv_w.shape), _resident(gate_bias.shape),
            _resident(wa.shape), _resident(wb.shape), _resident(wo.shape), _resident(ng.shape),
        ],
        out_specs=(pl.BlockSpec((tm, d), lambda i: (i, 0)), pl.BlockSpec((tm, d), lambda i: (i, 0)),
                   pl.BlockSpec((d, tm), lambda i: (0, i))),
        scratch_shapes=[pltpu.VMEM((tm + SUBLANES, cw), F32)],
        compiler_params=_params("parallel"),
        name="merge",
    )(proj, proj, proj, proj, proj, proj, proj, attn, x, conv_w, gate_bias, wa, wb, wo, ng)


def _sort_network(n):
    pairs = []

    def merge(lo, hi, r):
        step = r * 2
        if step < hi - lo:
            merge(lo, hi, step)
            merge(lo + r, hi, step)
            pairs.extend((i, i + r) for i in range(lo + r, hi - r, step))
        else:
            pairs.append((lo, lo + r))

    def sort(lo, hi):
        if hi - lo >= 1:
            mid = lo + (hi - lo) // 2
            sort(lo, mid)
            sort(mid + 1, hi)
            merge(lo, hi, 1)

    sort(0, n - 1)
    return pairs


def _compare_exchange(xs, i, j):
    a, b = xs[i], xs[j]
    if b is None:
        return
    if a is None:
        xs[i], xs[j] = b, None
        return
    xs[i], xs[j] = jnp.maximum(a, b), jnp.minimum(a, b)


def _sort_desc(xs):
    n = pl.next_power_of_2(len(xs))
    xs = list(xs) + [None] * (n - len(xs))
    for i, j in _sort_network(n):
        _compare_exchange(xs, i, j)
    return xs


def _top_per_column(s):
    n = s.shape[0] // SUBLANES
    xs = _sort_desc([s[SUBLANES * k:SUBLANES * (k + 1), :] for k in range(n)])
    r = SUBLANES // 2
    while r >= 1:
        xs = [jnp.maximum(xs[k], pltpu.roll(xs[n - 1 - k], SUBLANES - r, 0)) for k in range(n)]
        d = n // 2
        while d >= 1:
            for k in range(n):
                if k & d == 0:
                    _compare_exchange(xs, k, k + d)
            d //= 2
        r //= 2
    return xs


def _pscore_kernel(xn_ref, wq_ref, keys_ref, c1_ref, s2_ref, w1_ref, w2_ref, *, n_heads):
    nkeys, half = keys_ref.shape[1], keys_ref.shape[2]
    tm = xn_ref.shape[0]
    topk = PEER_TOPK
    qp = jnp.dot(xn_ref[...], wq_ref[...], preferred_element_type=F32)
    nt = (((1,), (1,)), ((), ()))
    sub = lax.broadcasted_iota(jnp.int32, (SUBLANES, tm), 0)

    tops = [[jnp.zeros((SUBLANES, tm), F32)] * (topk + 1) for _ in range(2)]
    for h in range(n_heads):
        for p, ref in enumerate((c1_ref, s2_ref)):
            c = (2 * h + p) * half
            s = _dot_nt_3pass(keys_ref[2 * h + p], qp[:, c:c + half])
            ref[h] = s
            col = _top_per_column(s)
            below = jnp.max(jnp.where(s < col[topk - 1][0:1, :], s, -jnp.inf), axis=0, keepdims=True)
            col = [pltpu.roll(x, h, 0) if h else x for x in col] + [below]
            tops[p] = [jnp.where(sub == h, col[k], tops[p][k]) for k in range(topk + 1)]
    a, b = tops

    cand = [a[p] + b[q] for p in range(topk + 1) for q in range(topk + 1)
            if (p + 1) * (q + 1) <= topk + 1]
    best = _sort_desc(cand)[:topk + 1]
    z = best[0] * 0.0
    for k in range(topk):
        z = z + jnp.exp(best[k] - best[0])
    tau = 0.5 * (best[topk - 1] + best[topk])
    half_inv_z = 0.5 / z

    for h in range(n_heads):
        row = slice(h, h + 1)
        s1 = c1_ref[h]
        w1_ref[h] = jnp.exp(s1 - a[0][row, :]) * half_inv_z[row, :]
        w2_ref[h] = jnp.exp(s2_ref[h] - b[0][row, :]).astype(w2_ref.dtype)
        c1_ref[h] = tau[row, :] - s1


def _pscore(xn, wq, keys, n_heads, tm):
    t, d = xn.shape
    nkeys = keys.shape[1]
    assert n_heads == SUBLANES and nkeys == SUBLANES * PEER_TOPK
    big = jax.ShapeDtypeStruct((n_heads, nkeys, t), F32)
    bspec = pl.BlockSpec((n_heads, nkeys, tm), lambda i: (0, 0, i))
    return pl.pallas_call(
        functools.partial(_pscore_kernel, n_heads=n_heads),
        out_shape=(big, big, big, jax.ShapeDtypeStruct(big.shape, BF16)),
        grid=(t // tm,),
        in_specs=[pl.BlockSpec((tm, d), lambda i: (i, 0)), _resident(wq.shape), _resident(keys.shape)],
        out_specs=(bspec, bspec, bspec, bspec),
        compiler_params=_params("parallel"),
        name="pscore",
    )(xn, wq, keys)


_GELU_C = 0.7978845608028654
_ACT_ROWS = 32
_GATE_ROWS = 32


def _pdense_kernel(xnt_ref, u_ref, v_ref, c1_ref, s2_ref, w1_ref, w2_ref, x1_ref, g_ref,
                   o_ref, gate_ref, a_ref, act_ref, *, n_heads):
    e = pl.program_id(1)
    n_chunks = pl.num_programs(1) - 1
    nkeys = s2_ref.shape[1]
    ec = u_ref.shape[0]
    tm = s2_ref.shape[2]
    tn = (((0,), (0,)), ((), ()))

    @pl.when(e == 0)
    def _():
        o_ref[...] = jnp.zeros_like(o_ref)
        act_ref[1] = jnp.zeros(act_ref.shape[1:], act_ref.dtype)

    groups = ec // nkeys
    slot = e % 2
    chunk = jnp.minimum(e, n_chunks - 1)
    row0 = (chunk * groups) % SUBLANES

    def gate_groups(g0, g1):
        gis = range(g0, g1)
        w1_rows = {gi: [w1_ref[h, pl.ds(row0 + gi, 1), :] for h in range(n_heads)] for gi in gis}
        c1_rows = {gi: [c1_ref[h, pl.ds(row0 + gi, 1), :] for h in range(n_heads)] for gi in gis}
        for lt in range(tm // LANES):
            ln = slice(lt * LANES, (lt + 1) * LANES)
            for r0 in range(0, nkeys, _GATE_ROWS):
                rs = slice(r0, r0 + _GATE_ROWS)
                accs = {gi: jnp.zeros((_GATE_ROWS, LANES), BF16) for gi in gis}
                for h in range(n_heads):
                    s2t = s2_ref[h, rs, ln]
                    w2t = w2_ref[h, rs, ln]
                    for gi in gis:
                        w = w1_rows[gi][h][:, ln].astype(BF16) * w2t
                        accs[gi] = accs[gi] + jnp.where(s2t >= c1_rows[gi][h][:, ln], w, jnp.zeros_like(w))
                for gi in gis:
                    gate_ref[gi * nkeys + r0:gi * nkeys + r0 + _GATE_ROWS, ln] = accs[gi]

    def act_rows(r0, r1):
        for r in range(r0, r1, _ACT_ROWS):
            rows = slice(r, r + _ACT_ROWS)
            a = a_ref[rows, :]
            th = jnp.tanh(a * (_GELU_C + (_GELU_C * 0.044715) * (a * a)))
            act_ref[slot, rows, :] = (a + a * th).astype(BF16) * gate_ref[rows, :]

    halves = 2
    for m in range(halves):
        rows = slice(m * ec // halves, (m + 1) * ec // halves)
        a_ref[rows, :] = jnp.dot(u_ref[rows, :], xnt_ref[...], preferred_element_type=F32)
        gate_groups(m * groups // halves, (m + 1) * groups // halves)
    o_ref[...] += lax.dot_general(act_ref[1 - slot], v_ref[...], tn, preferred_element_type=F32)
    act_rows(0, ec)

    @pl.when(e == n_chunks)
    def _():
        o_ref[...] = _rms_norm(x1_ref[...] + o_ref[...], g_ref[...])


def _pdense(xnt, u, v, c1, s2, w1, w2, x1, g, tm, ec):
    d, t = xnt.shape
    n_chunks = u.shape[0] // ec
    n_heads, nkeys, _ = s2.shape
    groups = ec // nkeys
    assert SUBLANES % groups == 0
    rows4 = pl.BlockSpec((n_heads, SUBLANES, tm),
                         lambda i, e: (0, jnp.minimum(e, n_chunks - 1) * groups // SUBLANES, i))
    once = pl.Buffered(1)
    return pl.pallas_call(
        functools.partial(_pdense_kernel, n_heads=n_heads),
        out_shape=jax.ShapeDtypeStruct((t, d), F32),
        grid=(t // tm, n_chunks + 1),
        in_specs=[
            pl.BlockSpec((d, tm), lambda i, e: (0, i), pipeline_mode=once),
            pl.BlockSpec((ec, d), lambda i, e: (jnp.minimum(e, n_chunks - 1), 0)),
            pl.BlockSpec((ec, d), lambda i, e: (jnp.maximum(e - 1, 0), 0)),
            rows4,
            pl.BlockSpec((n_heads, nkeys, tm), lambda i, e: (0, 0, i)),
            rows4,
            pl.BlockSpec((n_heads, nkeys, tm), lambda i, e: (0, 0, i)),
            pl.BlockSpec((tm, d), lambda i, e: (i, 0), pipeline_mode=once),
            pl.BlockSpec((1, d), lambda i, e: (0, 0)),
        ],
        out_specs=pl.BlockSpec((tm, d), lambda i, e: (i, 0), pipeline_mode=once),
        scratch_shapes=[pltpu.VMEM((ec, tm), BF16), pltpu.VMEM((ec, tm), F32),
                        pltpu.VMEM((2, ec, tm), BF16)],
        compiler_params=_params("parallel", "arbitrary", vmem_limit_bytes=BIG_TILE_VMEM_LIMIT_BYTES),
        name="pdense",
    )(xnt, u, v, c1, s2, w1, w2, x1, g)


def _layer(x, pos, an_g, w_in, gate_bias, conv_w, wa, wb, wo, fn_g, wq, keys, pu, pv, out_g, *,
           batch, seq, tiles):
    cw = wa.shape[0]
    aw = wb.shape[0]
    n_heads = aw // HEAD_DIM
    p_heads = keys.shape[0]

    inv_freq = 1.0 / (ROPE_THETA ** (jnp.arange(0, ROT_DIM, 2, dtype=F32) / ROT_DIM))
    invf = jnp.concatenate([inv_freq, inv_freq, jnp.zeros((HEAD_DIM - ROT_DIM,), F32)])[None, :]

    proj, q, k, v, kmean = _inproj(x, an_g[None, :], w_in.astype(BF16), pos, invf,
                                   tiles["inproj_tm"], aw, 3 * cw, n_heads)
    kmean = kmean.reshape(batch, seq // MOBA_BLOCK, aw)
    attn = _attn(q, k, v, kmean, batch, seq, n_heads, min(tiles["attn_heads"], n_heads))
    x1, xn, xnt = _merge(proj, attn, x, conv_w, gate_bias[None, :], wa.astype(BF16), wb.astype(BF16),
                    wo.astype(BF16), fn_g[None, :], seq, tiles["merge_tm"])
    keys2 = keys.reshape(p_heads * 2, keys.shape[2], keys.shape[3])
    c1, s2, w1, w2 = _pscore(xn, wq.astype(BF16), keys2, p_heads, tiles["pscore_tm"])
    return _pdense(xnt, pu.astype(BF16), pv.astype(BF16), c1, s2, w1, w2, x1, out_g[None, :],
                   tiles["pdense_tm"], tiles["pdense_ec"])


def _tiles(t, seq):
    return dict(
        inproj_tm=min(1024, t),
        attn_heads=8,
        merge_tm=min(256, seq),
        pscore_tm=min(512, t),
        pdense_tm=min(1024, t), pdense_ec=512,
    )


def kernel(x, positions, attn_norm_g, w_in, gate_bias, conv_w, w_branch_conv, w_branch_attn, w_out,
           ffn_norm_g, w_peer_query, peer_sub_keys, peer_u, peer_v, final_norm_g):
    b, s, d = x.shape
    depth = w_in.shape[0]
    assert depth == 1, "final norm is fused into the last layer's kernel"
    t = b * s
    xt = x.reshape(t, d)
    pos = positions.reshape(t, 1)
    out = _layer(xt, pos, attn_norm_g[0], w_in[0], gate_bias[0], conv_w[0], w_branch_conv[0],
                 w_branch_attn[0], w_out[0], ffn_norm_g[0], w_peer_query[0], peer_sub_keys[0],
                 peer_u[0], peer_v[0], final_norm_g, batch=b, seq=s, tiles=_tiles(t, s))
    return out.reshape(b, s, d)
```

```python
import functools

import jax
import jax.numpy as jnp
from jax import lax
from jax.experimental import pallas as pl
from jax.experimental.pallas import tpu as pltpu

F32 = jnp.float32
BF16 = jnp.bfloat16

HEAD_DIM = 128
ROT_DIM = HEAD_DIM // 4
ROPE_THETA = 500000.0
MOBA_BLOCK = 256
MOBA_TOPK = 3
PEER_TOPK = 16
RMS_EPS = 1e-6

LANES = 128
SUBLANES = 8
VMEM_CAPACITY_BYTES = 64 * 1024 * 1024
VMEM_LIMIT_BYTES = VMEM_CAPACITY_BYTES - 8 * 1024 * 1024
BIG_TILE_VMEM_LIMIT_BYTES = VMEM_CAPACITY_BYTES - 4 * 1024 * 1024

NEG_BIG = -1e30


def _params(*sem, vmem_limit_bytes=VMEM_LIMIT_BYTES):
    return pltpu.CompilerParams(dimension_semantics=sem, vmem_limit_bytes=vmem_limit_bytes)


def _resident(shape):
    zeros = (0,) * len(shape)
    return pl.BlockSpec(shape, lambda *_: zeros, pipeline_mode=pl.Buffered(1))


def _dot_nt_3pass(a, b):
    nt = (((1,), (1,)), ((), ()))
    a_hi, b_hi = a.astype(BF16), b.astype(BF16)
    a_lo = (a - a_hi.astype(F32)).astype(BF16)
    b_lo = (b - b_hi.astype(F32)).astype(BF16)
    dot = lambda x, y: lax.dot_general(x, y, nt, preferred_element_type=F32)
    return dot(a_hi, b_hi) + (dot(a_hi, b_lo) + dot(a_lo, b_hi))


def _rms_norm(x, g):
    ms = jnp.mean(x * x, axis=-1, keepdims=True)
    return x * lax.rsqrt(ms + RMS_EPS) * g


def _inproj_kernel(x_ref, g_ref, w_ref, pos_ref, invf_ref, o_ref, q_ref, k_ref, v_ref, km_ref, h_ref,
                   *, q_tile, n_heads):
    j = pl.program_id(1)

    @pl.when(j == 0)
    def _():
        h_ref[...] = _rms_norm(x_ref[...], g_ref[...]).astype(BF16)

    o_ref[...] = jnp.dot(h_ref[...], w_ref[...], preferred_element_type=F32)

    def rotary():
        half = ROT_DIM // 2
        ang = pos_ref[...].astype(F32) * invf_ref[...]
        cos = jnp.cos(ang)
        sin = jnp.sin(ang)
        lane = lax.broadcasted_iota(jnp.int32, ang.shape, 1)
        sin_lo = jnp.where(lane < half, -sin, 0.0)
        sin_hi = jnp.where((lane >= half) & (lane < ROT_DIM), sin, 0.0)
        return lambda x: (x * cos + pltpu.roll(x, HEAD_DIM - half, 1) * sin_lo
                          + pltpu.roll(x, half, 1) * sin_hi)

    heads = [slice(h * HEAD_DIM, (h + 1) * HEAD_DIM) for h in range(n_heads)]

    @pl.when(j == q_tile)
    def _():
        rot = rotary()
        for hs in heads:
            q_ref[:, hs] = rot(o_ref[:, hs])

    @pl.when(j == q_tile + 1)
    def _():
        rot = rotary()
        for hs in heads:
            kr = rot(o_ref[:, hs])
            k_ref[:, hs] = kr.astype(BF16)
            for b in range(km_ref.shape[0]):
                km_ref[b, :, hs] = jnp.mean(kr[b * MOBA_BLOCK:(b + 1) * MOBA_BLOCK, :], axis=0, keepdims=True)

    @pl.when(j == q_tile + 2)
    def _():
        v_ref[...] = o_ref[...].astype(BF16)


def _inproj(x, g, w, pos, invf, tm, tn, q_col, n_heads):
    t, d = x.shape
    n = w.shape[1]
    aw = n_heads * HEAD_DIM
    assert tn == aw and q_col % tn == 0 and tm % MOBA_BLOCK == 0
    row = lambda i, j: (i, 0)
    return pl.pallas_call(
        functools.partial(_inproj_kernel, q_tile=q_col // tn, n_heads=n_heads),
        out_shape=(
            jax.ShapeDtypeStruct((t, n), F32),
            jax.ShapeDtypeStruct((t, aw), F32),
            jax.ShapeDtypeStruct((t, aw), BF16),
            jax.ShapeDtypeStruct((t, aw), BF16),
            jax.ShapeDtypeStruct((t // MOBA_BLOCK, 1, aw), F32),
        ),
        grid=(t // tm, n // tn),
        in_specs=[
            pl.BlockSpec((tm, d), row),
            pl.BlockSpec((1, d), lambda i, j: (0, 0)),
            pl.BlockSpec((d, tn), lambda i, j: (0, j)),
            pl.BlockSpec((tm, 1), row),
            pl.BlockSpec((1, HEAD_DIM), lambda i, j: (0, 0)),
        ],
        out_specs=(
            pl.BlockSpec((tm, tn), lambda i, j: (i, j)),
            pl.BlockSpec((tm, aw), row),
            pl.BlockSpec((tm, aw), row),
            pl.BlockSpec((tm, aw), row),
            pl.BlockSpec((tm // MOBA_BLOCK, 1, aw), lambda i, j: (i, 0, 0)),
        ),
        scratch_shapes=[pltpu.VMEM((tm, d), BF16)],
        compiler_params=_params("parallel", "arbitrary", vmem_limit_bytes=BIG_TILE_VMEM_LIMIT_BYTES),
        name="inproj",
    )(x, g, w, pos, invf)


_LOG2E = 1.4426950408889634


def _attn_kernel(q_ref, k_ref, v_ref, km_ref, o_ref, *, scale, heads):
    i = pl.program_id(2)
    blk = MOBA_BLOCK
    span = 2 * blk
    hd = HEAD_DIM
    nb = km_ref.shape[1]
    c = scale * _LOG2E
    nt = (((1,), (1,)), ((), ()))
    pair = lax.shift_right_logical(i, 1)
    odd = i - 2 * pair
    ones = jnp.ones((span, LANES), BF16)
    lane = lax.broadcasted_iota(jnp.int32, (span, LANES), 1)
    upper = jnp.where(lax.broadcasted_iota(jnp.int32, (span, LANES), 0) >= blk, 1, 0)
    row = lax.broadcasted_iota(jnp.int32, (blk, span), 0)
    colv = lax.broadcasted_iota(jnp.int32, (blk, span), 1)
    bidx = lax.broadcasted_iota(jnp.int32, (nb, blk), 0)

    def kv_pair(p, hs):
        start = pl.multiple_of(p * span, span)
        onehot = jnp.where(lane == 2 * p + upper, 1.0, 0.0).astype(BF16)
        k_aug = jnp.concatenate([k_ref[pl.ds(start, span), hs], onehot], axis=1)
        v_aug = jnp.concatenate([v_ref[pl.ds(start, span), hs], ones], axis=1)
        return k_aug, v_aug

    q_augs = []
    init = []
    for h in range(heads):
        hs = slice(h * hd, (h + 1) * hd)
        q = q_ref[:, hs]
        gate = _dot_nt_3pass(km_ref[0, :, hs], q)
        gate = jnp.where(bidx < i, gate, -jnp.inf)
        sel = jnp.zeros(gate.shape, F32)
        for r in range(MOBA_TOPK):
            top = jnp.max(gate, axis=0, keepdims=True)
            first = jnp.min(jnp.where(gate == top, bidx, nb), axis=0, keepdims=True)
            hit = bidx == first
            sel = jnp.where(hit & (r < i), 1.0, sel)
            gate = jnp.where(hit, -jnp.inf, gate)
        bias_t = jnp.where((sel > 0.0) | (bidx == i), 0.0, NEG_BIG)
        bias_t = jnp.concatenate([bias_t, jnp.zeros((LANES - nb, blk), F32)], axis=0)
        bias = bias_t.T.astype(BF16)
        q_aug = jnp.concatenate([q.astype(BF16), bias], axis=1)
        q_augs.append(q_aug)

        k_aug, v_aug = kv_pair(pair, hs)
        s = lax.dot_general(q_aug, k_aug, nt, preferred_element_type=F32) * c
        s = jnp.where(colv - row > odd * blk, -jnp.inf, s)
        m0 = jnp.max(s, axis=-1, keepdims=True)
        p = jnp.exp2(s - m0).astype(BF16)
        init.append((m0, jnp.dot(p, v_aug, preferred_element_type=F32)))

    def body(j, carry):
        out = []
        for h in range(heads):
            hs = slice(h * hd, (h + 1) * hd)
            m, acc = carry[h]
            k_aug, v_aug = kv_pair(j, hs)
            s = lax.dot_general(q_augs[h], k_aug, nt, preferred_element_type=F32) * c
            m_new = jnp.maximum(m, jnp.max(s, axis=-1, keepdims=True))
            p = jnp.exp2(s - m_new).astype(BF16)
            acc = jnp.exp2(m - m_new) * acc + jnp.dot(p, v_aug, preferred_element_type=F32)
            out.append((m_new, acc))
        return tuple(out)

    final = lax.fori_loop(0, pair, body, tuple(init))
    for h in range(heads):
        acc = final[h][1]
        o_ref[:, h * hd:(h + 1) * hd] = (acc[:, :hd] / acc[:, hd:]).astype(o_ref.dtype)


def _attn(q, k, v, kmean, batch, seq, n_heads, heads):
    t, w = q.shape
    nq = seq // MOBA_BLOCK
    blk = MOBA_BLOCK
    hw = heads * HEAD_DIM
    assert nq % 2 == 0 and nq <= LANES, "key blocks are visited in pairs; mask columns fit one lane tile"
    return pl.pallas_call(
        functools.partial(_attn_kernel, scale=HEAD_DIM ** -0.5, heads=heads),
        out_shape=jax.ShapeDtypeStruct((t, w), BF16),
        grid=(batch, n_heads // heads, nq),
        in_specs=[
            pl.BlockSpec((blk, hw), lambda b, g, i: (b * nq + i, g)),
            pl.BlockSpec((seq, hw), lambda b, g, i: (b, g)),
            pl.BlockSpec((seq, hw), lambda b, g, i: (b, g)),
            pl.BlockSpec((1, nq, hw), lambda b, g, i: (b, 0, g)),
        ],
        out_specs=pl.BlockSpec((blk, hw), lambda b, g, i: (b * nq + i, g)),
        compiler_params=_params("parallel", "parallel", "arbitrary"),
        name="attn",
    )(q, k, v, kmean)


def _merge_kernel(cb_ref, cc_ref, cx_ref, pc_ref, px_ref, ga_ref, gb_ref, at_ref, x_ref,
                  cw_ref, bias_ref, wa_ref, wb_ref, wo_ref, ng_ref,
                  x1_ref, xn_ref, xnt_ref, z_ref, *, seq):
    tm = cb_ref.shape[0]
    d = x_ref.shape[1]
    first = (pl.program_id(0) * tm) % seq == 0
    z = cc_ref[...] * cx_ref[...]
    z_prev = jnp.where(first, 0.0, pc_ref[...] * px_ref[...])
    z_ref[0:SUBLANES, :] = z_prev
    z_ref[SUBLANES:, :] = z
    cw = cw_ref[...]
    conv = (cw[2:3, :] * z
            + cw[1:2, :] * z_ref[SUBLANES - 1:SUBLANES - 1 + tm, :]
            + cw[0:1, :] * z_ref[SUBLANES - 2:SUBLANES - 2 + tm, :])
    u = (cb_ref[...] * conv).astype(BF16)
    y_conv = jnp.dot(u, wa_ref[...], preferred_element_type=F32)
    y_attn = jnp.dot(at_ref[...], wb_ref[...], preferred_element_type=F32)
    bias = bias_ref[...]
    merged = (jax.nn.sigmoid(ga_ref[...] + bias[:, :d]) * y_conv
              + jax.nn.sigmoid(gb_ref[...] + bias[:, d:]) * y_attn)
    x1 = x_ref[...] + jnp.dot(merged.astype(BF16), wo_ref[...], preferred_element_type=F32)
    x1_ref[...] = x1
    xn = _rms_norm(x1, ng_ref[...])
    xn_ref[...] = xn.astype(BF16)
    xnt_ref[...] = xn.T.astype(BF16)


def _merge(proj, attn, x, conv_w, gate_bias, wa, wb, wo, ng, seq, tm):
    t, d = x.shape
    cw = wa.shape[0]
    aw = wb.shape[0]
    gcol = (3 * cw + 3 * aw) // d
    halo = lambda c: pl.BlockSpec(
        (SUBLANES, cw), lambda i: (jnp.maximum(i * (tm // SUBLANES) - 1, 0), c))
    return pl.pallas_call(
        functools.partial(_merge_kernel, seq=seq),
        out_shape=(jax.ShapeDtypeStruct((t, d), F32), jax.ShapeDtypeStruct((t, d), BF16),
                   jax.ShapeDtypeStruct((d, t), BF16)),
        grid=(t // tm,),
        in_specs=[
            pl.BlockSpec((tm, cw), lambda i: (i, 0)),
            pl.BlockSpec((tm, cw), lambda i: (i, 1)),
            pl.BlockSpec((tm, cw), lambda i: (i, 2)),
            halo(1), halo(2),
            pl.BlockSpec((tm, d), lambda i: (i, gcol)),
            pl.BlockSpec((tm, d), lambda i: (i, gcol + 1)),
            pl.BlockSpec((tm, aw), lambda i: (i, 0)),
            pl.BlockSpec((tm, d), lambda i: (i, 0)),
            _resident(conv_w.shape), _resident(gate_bias.shape),
            _resident(wa.shape), _resident(wb.shape), _resident(wo.shape), _resident(ng.shape),
        ],
        out_specs=(pl.BlockSpec((tm, d), lambda i: (i, 0)), pl.BlockSpec((tm, d), lambda i: (i, 0)),
                   pl.BlockSpec((d, tm), lambda i: (0, i))),
        scratch_shapes=[pltpu.VMEM((tm + SUBLANES, cw), F32)],
        compiler_params=_params("parallel"),
        name="merge",
    )(proj, proj, proj, proj, proj, proj, proj, attn, x, conv_w, gate_bias, wa, wb, wo, ng)


def _sort_network(n):
    pairs = []

    def merge(lo, hi, r):
        step = r * 2
        if step < hi - lo:
            merge(lo, hi, step)
            merge(lo + r, hi, step)
            pairs.extend((i, i + r) for i in range(lo + r, hi - r, step))
        else:
            pairs.append((lo, lo + r))

    def sort(lo, hi):
        if hi - lo >= 1:
            mid = lo + (hi - lo) // 2
            sort(lo, mid)
            sort(mid + 1, hi)
            merge(lo, hi, 1)

    sort(0, n - 1)
    return pairs


def _compare_exchange(xs, i, j):
    a, b = xs[i], xs[j]
    if b is None:
        return
    if a is None:
        xs[i], xs[j] = b, None
        return
    xs[i], xs[j] = jnp.maximum(a, b), jnp.minimum(a, b)


def _sort_desc(xs):
    n = pl.next_power_of_2(len(xs))
    xs = list(xs) + [None] * (n - len(xs))
    for i, j in _sort_network(n):
        _compare_exchange(xs, i, j)
    return xs


def _top_per_column(s):
    n = s.shape[0] // SUBLANES
    xs = _sort_desc([s[SUBLANES * k:SUBLANES * (k + 1), :] for k in range(n)])
    r = SUBLANES // 2
    while r >= 1:
        xs = [jnp.maximum(xs[k], pltpu.roll(xs[n - 1 - k], SUBLANES - r, 0)) for k in range(n)]
        d = n // 2
        while d >= 1:
            for k in range(n):
                if k & d == 0:
                    _compare_exchange(xs, k, k + d)
            d //= 2
        r //= 2
    return xs


def _pscore_kernel(xn_ref, wq_ref, keys_ref, c1_ref, s2_ref, w1_ref, w2_ref, *, n_heads):
    nkeys, half = keys_ref.shape[1], keys_ref.shape[2]
    tm = xn_ref.shape[0]
    topk = PEER_TOPK
    qp = jnp.dot(xn_ref[...], wq_ref[...], preferred_element_type=F32)
    nt = (((1,), (1,)), ((), ()))
    sub = lax.broadcasted_iota(jnp.int32, (SUBLANES, tm), 0)

    tops = [[jnp.zeros((SUBLANES, tm), F32)] * (topk + 1) for _ in range(2)]
    for h in range(n_heads):
        for p, ref in enumerate((c1_ref, s2_ref)):
            c = (2 * h + p) * half
            s = _dot_nt_3pass(keys_ref[2 * h + p], qp[:, c:c + half])
            ref[h] = s
            col = _top_per_column(s)
            below = jnp.max(jnp.where(s < col[topk - 1][0:1, :], s, -jnp.inf), axis=0, keepdims=True)
            col = [pltpu.roll(x, h, 0) if h else x for x in col] + [below]
            tops[p] = [jnp.where(sub == h, col[k], tops[p][k]) for k in range(topk + 1)]
    a, b = tops

    cand = [a[p] + b[q] for p in range(topk + 1) for q in range(topk + 1)
            if (p + 1) * (q + 1) <= topk + 1]
    best = _sort_desc(cand)[:topk + 1]
    z = best[0] * 0.0
    for k in range(topk):
        z = z + jnp.exp(best[k] - best[0])
    tau = 0.5 * (best[topk - 1] + best[topk])
    half_inv_z = (0.5 / _GELU_C) / z

    for h in range(n_heads):
        row = slice(h, h + 1)
        s1 = c1_ref[h]
        w1_ref[h] = jnp.exp(s1 - a[0][row, :]) * half_inv_z[row, :]
        w2_ref[h] = jnp.exp(s2_ref[h] - b[0][row, :]).astype(w2_ref.dtype)
        c1_ref[h] = tau[row, :] - s1


def _pscore(xn, wq, keys, n_heads, tm):
    t, d = xn.shape
    nkeys = keys.shape[1]
    assert n_heads == SUBLANES and nkeys == SUBLANES * PEER_TOPK
    big = jax.ShapeDtypeStruct((n_heads, nkeys, t), F32)
    bspec = pl.BlockSpec((n_heads, nkeys, tm), lambda i: (0, 0, i))
    return pl.pallas_call(
        functools.partial(_pscore_kernel, n_heads=n_heads),
        out_shape=(big, big, big, jax.ShapeDtypeStruct(big.shape, BF16)),
        grid=(t // tm,),
        in_specs=[pl.BlockSpec((tm, d), lambda i: (i, 0)), _resident(wq.shape), _resident(keys.shape)],
        out_specs=(bspec, bspec, bspec, bspec),
        compiler_params=_params("parallel"),
        name="pscore",
    )(xn, wq, keys)


_GELU_C = 0.7978845608028654
_GELU_K = (_GELU_C * 0.044715) ** (1.0 / 3.0)
_ACT_ROWS = 32
_GATE_ROWS = 32


def _pdense_kernel(xnt_ref, u_ref, v_ref, c1_ref, s2_ref, w1_ref, w2_ref, x1_ref, g_ref,
                   o_ref, gate_ref, a_ref, act_ref, *, n_heads):
    e = pl.program_id(1)
    n_chunks = pl.num_programs(1) - 1
    nkeys = s2_ref.shape[1]
    ec = u_ref.shape[0]
    tm = s2_ref.shape[2]
    tn = (((0,), (0,)), ((), ()))

    @pl.when(e == 0)
    def _():
        o_ref[...] = jnp.zeros_like(o_ref)
        act_ref[1] = jnp.zeros(act_ref.shape[1:], act_ref.dtype)

    groups = ec // nkeys
    slot = e % 2
    chunk = jnp.minimum(e, n_chunks - 1)
    row0 = (chunk * groups) % SUBLANES

    def gate_groups(g0, g1):
        gis = range(g0, g1)
        w1_rows = {gi: [w1_ref[h, pl.ds(row0 + gi, 1), :] for h in range(n_heads)] for gi in gis}
        c1_rows = {gi: [c1_ref[h, pl.ds(row0 + gi, 1), :] for h in range(n_heads)] for gi in gis}
        for lt in range(tm // LANES):
            ln = slice(lt * LANES, (lt + 1) * LANES)
            for r0 in range(0, nkeys, _GATE_ROWS):
                rs = slice(r0, r0 + _GATE_ROWS)
                accs = {gi: jnp.zeros((_GATE_ROWS, LANES), BF16) for gi in gis}
                for h in range(n_heads):
                    s2t = s2_ref[h, rs, ln]
                    w2t = w2_ref[h, rs, ln]
                    for gi in gis:
                        w = w1_rows[gi][h][:, ln].astype(BF16) * w2t
                        accs[gi] = accs[gi] + jnp.where(s2t >= c1_rows[gi][h][:, ln], w, jnp.zeros_like(w))
                for gi in gis:
                    gate_ref[gi * nkeys + r0:gi * nkeys + r0 + _GATE_ROWS, ln] = accs[gi]

    def act_rows(r0, r1):
        for r in range(r0, r1, _ACT_ROWS):
            rows = slice(r, r + _ACT_ROWS)
            a = a_ref[rows, :]
            p = (a * _GELU_C).astype(BF16)
            q = (a * _GELU_K).astype(BF16)
            th = jnp.tanh(p + q * q * q)
            act_ref[slot, rows, :] = (p + p * th) * gate_ref[rows, :]

    halves = 2
    for m in range(halves):
        rows = slice(m * ec // halves, (m + 1) * ec // halves)
        a_ref[rows, :] = jnp.dot(u_ref[rows, :], xnt_ref[...], preferred_element_type=F32)
        gate_groups(m * groups // halves, (m + 1) * groups // halves)
    o_ref[...] += lax.dot_general(act_ref[1 - slot], v_ref[...], tn, preferred_element_type=F32)
    act_rows(0, ec)

    @pl.when(e == n_chunks)
    def _():
        o_ref[...] = _rms_norm(x1_ref[...] + o_ref[...], g_ref[...])


def _pdense(xnt, u, v, c1, s2, w1, w2, x1, g, tm, ec):
    d, t = xnt.shape
    n_chunks = u.shape[0] // ec
    n_heads, nkeys, _ = s2.shape
    groups = ec // nkeys
    assert SUBLANES % groups == 0
    rows4 = pl.BlockSpec((n_heads, SUBLANES, tm),
                         lambda i, e: (0, jnp.minimum(e, n_chunks - 1) * groups // SUBLANES, i))
    once = pl.Buffered(1)
    return pl.pallas_call(
        functools.partial(_pdense_kernel, n_heads=n_heads),
        out_shape=jax.ShapeDtypeStruct((t, d), F32),
        grid=(t // tm, n_chunks + 1),
        in_specs=[
            pl.BlockSpec((d, tm), lambda i, e: (0, i), pipeline_mode=once),
            pl.BlockSpec((ec, d), lambda i, e: (jnp.minimum(e, n_chunks - 1), 0)),
            pl.BlockSpec((ec, d), lambda i, e: (jnp.maximum(e - 1, 0), 0)),
            rows4,
            pl.BlockSpec((n_heads, nkeys, tm), lambda i, e: (0, 0, i)),
            rows4,
            pl.BlockSpec((n_heads, nkeys, tm), lambda i, e: (0, 0, i)),
            pl.BlockSpec((tm, d), lambda i, e: (i, 0), pipeline_mode=once),
            pl.BlockSpec((1, d), lambda i, e: (0, 0)),
        ],
        out_specs=pl.BlockSpec((tm, d), lambda i, e: (i, 0), pipeline_mode=once),
        scratch_shapes=[pltpu.VMEM((ec, tm), BF16), pltpu.VMEM((ec, tm), F32),
                        pltpu.VMEM((2, ec, tm), BF16)],
        compiler_params=_params("parallel", "arbitrary", vmem_limit_bytes=BIG_TILE_VMEM_LIMIT_BYTES),
        name="pdense",
    )(xnt, u, v, c1, s2, w1, w2, x1, g)


def _layer(x, pos, an_g, w_in, gate_bias, conv_w, wa, wb, wo, fn_g, wq, keys, pu, pv, out_g, *,
           batch, seq, tiles):
    cw = wa.shape[0]
    aw = wb.shape[0]
    n_heads = aw // HEAD_DIM
    p_heads = keys.shape[0]

    inv_freq = 1.0 / (ROPE_THETA ** (jnp.arange(0, ROT_DIM, 2, dtype=F32) / ROT_DIM))
    invf = jnp.concatenate([inv_freq, inv_freq, jnp.zeros((HEAD_DIM - ROT_DIM,), F32)])[None, :]

    proj, q, k, v, kmean = _inproj(x, an_g[None, :], w_in.astype(BF16), pos, invf,
                                   tiles["inproj_tm"], aw, 3 * cw, n_heads)
    kmean = kmean.reshape(batch, seq // MOBA_BLOCK, aw)
    attn = _attn(q, k, v, kmean, batch, seq, n_heads, min(tiles["attn_heads"], n_heads))
    x1, xn, xnt = _merge(proj, attn, x, conv_w, gate_bias[None, :], wa.astype(BF16), wb.astype(BF16),
                    wo.astype(BF16), fn_g[None, :], seq, tiles["merge_tm"])
    keys2 = keys.reshape(p_heads * 2, keys.shape[2], keys.shape[3])
    c1, s2, w1, w2 = _pscore(xn, wq.astype(BF16), keys2, p_heads, tiles["pscore_tm"])
    return _pdense(xnt, pu.astype(BF16), pv.astype(BF16), c1, s2, w1, w2, x1, out_g[None, :],
                   tiles["pdense_tm"], tiles["pdense_ec"])


def _tiles(t, seq):
    return dict(
        inproj_tm=min(1024, t),
        attn_heads=8,
        merge_tm=min(256, seq),
        pscore_tm=min(512, t),
        pdense_tm=min(1024, t), pdense_ec=512,
    )


def kernel(x, positions, attn_norm_g, w_in, gate_bias, conv_w, w_branch_conv, w_branch_attn, w_out,
           ffn_norm_g, w_peer_query, peer_sub_keys, peer_u, peer_v, final_norm_g):
    b, s, d = x.shape
    depth = w_in.shape[0]
    assert depth == 1, "final norm is fused into the last layer's kernel"
    t = b * s
    xt = x.reshape(t, d)
    pos = positions.reshape(t, 1)
    out = _layer(xt, pos, attn_norm_g[0], w_in[0], gate_bias[0], conv_w[0], w_branch_conv[0],
                 w_branch_attn[0], w_out[0], ffn_norm_g[0], w_peer_query[0], peer_sub_keys[0],
                 peer_u[0], peer_v[0], final_norm_g, batch=b, seq=s, tiles=_tiles(t, s))
    return out.reshape(b, s, d)
```

```python
import functools

import jax
import jax.numpy as jnp
from jax import lax
from jax.experimental import pallas as pl
from jax.experimental.pallas import tpu as pltpu

F32 = jnp.float32
BF16 = jnp.bfloat16

HEAD_DIM = 128
ROT_DIM = HEAD_DIM // 4
ROPE_THETA = 500000.0
MOBA_BLOCK = 256
MOBA_TOPK = 3
PEER_TOPK = 16
RMS_EPS = 1e-6

LANES = 128
SUBLANES = 8
VMEM_CAPACITY_BYTES = 64 * 1024 * 1024
VMEM_LIMIT_BYTES = VMEM_CAPACITY_BYTES - 8 * 1024 * 1024
BIG_TILE_VMEM_LIMIT_BYTES = VMEM_CAPACITY_BYTES - 4 * 1024 * 1024

NEG_BIG = -1e30


def _params(*sem, vmem_limit_bytes=VMEM_LIMIT_BYTES):
    return pltpu.CompilerParams(dimension_semantics=sem, vmem_limit_bytes=vmem_limit_bytes)


def _resident(shape):
    zeros = (0,) * len(shape)
    return pl.BlockSpec(shape, lambda *_: zeros, pipeline_mode=pl.Buffered(1))


def _dot_nt_3pass(a, b):
    nt = (((1,), (1,)), ((), ()))
    a_hi, b_hi = a.astype(BF16), b.astype(BF16)
    a_lo = (a - a_hi.astype(F32)).astype(BF16)
    b_lo = (b - b_hi.astype(F32)).astype(BF16)
    dot = lambda x, y: lax.dot_general(x, y, nt, preferred_element_type=F32)
    return dot(a_hi, b_hi) + (dot(a_hi, b_lo) + dot(a_lo, b_hi))


def _rms_norm(x, g):
    ms = jnp.mean(x * x, axis=-1, keepdims=True)
    return x * lax.rsqrt(ms + RMS_EPS) * g


def _inproj_kernel(x_ref, g_ref, w_ref, pos_ref, invf_ref, o_ref, q_ref, k_ref, v_ref, km_ref, h_ref,
                   *, q_tile, n_heads):
    j = pl.program_id(1)

    @pl.when(j == 0)
    def _():
        h_ref[...] = _rms_norm(x_ref[...], g_ref[...]).astype(BF16)

    o_ref[...] = jnp.dot(h_ref[...], w_ref[...], preferred_element_type=F32)

    def rotary():
        half = ROT_DIM // 2
        ang = pos_ref[...].astype(F32) * invf_ref[...]
        cos = jnp.cos(ang)
        sin = jnp.sin(ang)
        lane = lax.broadcasted_iota(jnp.int32, ang.shape, 1)
        sin_lo = jnp.where(lane < half, -sin, 0.0)
        sin_hi = jnp.where((lane >= half) & (lane < ROT_DIM), sin, 0.0)
        return lambda x: (x * cos + pltpu.roll(x, HEAD_DIM - half, 1) * sin_lo
                          + pltpu.roll(x, half, 1) * sin_hi)

    heads = [slice(h * HEAD_DIM, (h + 1) * HEAD_DIM) for h in range(n_heads)]

    @pl.when(j == q_tile)
    def _():
        rot = rotary()
        for hs in heads:
            q_ref[:, hs] = rot(o_ref[:, hs])

    @pl.when(j == q_tile + 1)
    def _():
        rot = rotary()
        for hs in heads:
            kr = rot(o_ref[:, hs])
            k_ref[:, hs] = kr.astype(BF16)
            for b in range(km_ref.shape[0]):
                km_ref[b, :, hs] = jnp.mean(kr[b * MOBA_BLOCK:(b + 1) * MOBA_BLOCK, :], axis=0, keepdims=True)

    @pl.when(j == q_tile + 2)
    def _():
        v_ref[...] = o_ref[...].astype(BF16)


def _inproj(x, g, w, pos, invf, tm, tn, q_col, n_heads):
    t, d = x.shape
    n = w.shape[1]
    aw = n_heads * HEAD_DIM
    assert tn == aw and q_col % tn == 0 and tm % MOBA_BLOCK == 0
    row = lambda i, j: (i, 0)
    return pl.pallas_call(
        functools.partial(_inproj_kernel, q_tile=q_col // tn, n_heads=n_heads),
        out_shape=(
            jax.ShapeDtypeStruct((t, n), F32),
            jax.ShapeDtypeStruct((t, aw), F32),
            jax.ShapeDtypeStruct((t, aw), BF16),
            jax.ShapeDtypeStruct((t, aw), BF16),
            jax.ShapeDtypeStruct((t // MOBA_BLOCK, 1, aw), F32),
        ),
        grid=(t // tm, n // tn),
        in_specs=[
            pl.BlockSpec((tm, d), row),
            pl.BlockSpec((1, d), lambda i, j: (0, 0)),
            pl.BlockSpec((d, tn), lambda i, j: (0, j)),
            pl.BlockSpec((tm, 1), row),
            pl.BlockSpec((1, HEAD_DIM), lambda i, j: (0, 0)),
        ],
        out_specs=(
            pl.BlockSpec((tm, tn), lambda i, j: (i, j)),
            pl.BlockSpec((tm, aw), row),
            pl.BlockSpec((tm, aw), row),
            pl.BlockSpec((tm, aw), row),
            pl.BlockSpec((tm // MOBA_BLOCK, 1, aw), lambda i, j: (i, 0, 0)),
        ),
        scratch_shapes=[pltpu.VMEM((tm, d), BF16)],
        compiler_params=_params("parallel", "arbitrary", vmem_limit_bytes=BIG_TILE_VMEM_LIMIT_BYTES),
        name="inproj",
    )(x, g, w, pos, invf)


_LOG2E = 1.4426950408889634


def _attn_kernel(q_ref, k_ref, v_ref, km_ref, o_ref, *, scale, heads):
    i = pl.program_id(2)
    blk = MOBA_BLOCK
    span = 2 * blk
    hd = HEAD_DIM
    nb = km_ref.shape[1]
    c = scale * _LOG2E
    nt = (((1,), (1,)), ((), ()))
    pair = lax.shift_right_logical(i, 1)
    odd = i - 2 * pair
    lane = lax.broadcasted_iota(jnp.int32, (span, LANES), 1)
    upper = jnp.where(lax.broadcasted_iota(jnp.int32, (span, LANES), 0) >= blk, 1, 0)
    row = lax.broadcasted_iota(jnp.int32, (blk, span), 0)
    colv = lax.broadcasted_iota(jnp.int32, (blk, span), 1)
    bidx = lax.broadcasted_iota(jnp.int32, (nb, blk), 0)

    def kv_pair(p, hs):
        start = pl.multiple_of(p * span, span)
        onehot = jnp.where(lane == 2 * p + upper, 1.0, 0.0).astype(BF16)
        k_aug = jnp.concatenate([k_ref[pl.ds(start, span), hs], onehot], axis=1)
        return k_aug, v_ref[pl.ds(start, span), hs]

    q_augs = []
    init = []
    for h in range(heads):
        hs = slice(h * hd, (h + 1) * hd)
        q = q_ref[:, hs]
        gate = _dot_nt_3pass(km_ref[0, :, hs], q)
        gate = jnp.where(bidx < i, gate, -jnp.inf)
        sel = jnp.zeros(gate.shape, F32)
        for r in range(MOBA_TOPK):
            top = jnp.max(gate, axis=0, keepdims=True)
            first = jnp.min(jnp.where(gate == top, bidx, nb), axis=0, keepdims=True)
            hit = bidx == first
            sel = jnp.where(hit & (r < i), 1.0, sel)
            gate = jnp.where(hit, -jnp.inf, gate)
        bias_t = jnp.where((sel > 0.0) | (bidx == i), 0.0, NEG_BIG)
        bias_t = jnp.concatenate([bias_t, jnp.zeros((LANES - nb, blk), F32)], axis=0)
        bias = bias_t.T.astype(BF16)
        q_aug = jnp.concatenate([q.astype(BF16), bias], axis=1)
        q_augs.append(q_aug)

        k_aug, v_pair = kv_pair(pair, hs)
        s = lax.dot_general(q_aug, k_aug, nt, preferred_element_type=F32) * c
        s = jnp.where(colv - row > odd * blk, -jnp.inf, s)
        m0 = jnp.max(s, axis=-1, keepdims=True)
        p = jnp.exp2(s - m0)
        l0 = jnp.sum(p, axis=-1, keepdims=True)
        init.append((m0, l0, jnp.dot(p.astype(BF16), v_pair, preferred_element_type=F32)))

    def body(j, carry):
        out = []
        for h in range(heads):
            hs = slice(h * hd, (h + 1) * hd)
            m, l, acc = carry[h]
            k_aug, v_pair = kv_pair(j, hs)
            s = lax.dot_general(q_augs[h], k_aug, nt, preferred_element_type=F32) * c
            m_new = jnp.maximum(m, jnp.max(s, axis=-1, keepdims=True))
            p = jnp.exp2(s - m_new)
            alpha = jnp.exp2(m - m_new)
            l = alpha * l + jnp.sum(p, axis=-1, keepdims=True)
            acc = alpha * acc + jnp.dot(p.astype(BF16), v_pair, preferred_element_type=F32)
            out.append((m_new, l, acc))
        return tuple(out)

    final = lax.fori_loop(0, pair, body, tuple(init))
    for h in range(heads):
        _, l, acc = final[h]
        o_ref[:, h * hd:(h + 1) * hd] = (acc / l).astype(o_ref.dtype)


def _attn(q, k, v, kmean, batch, seq, n_heads, heads):
    t, w = q.shape
    nq = seq // MOBA_BLOCK
    blk = MOBA_BLOCK
    hw = heads * HEAD_DIM
    assert nq % 2 == 0 and nq <= LANES, "key blocks are visited in pairs; mask columns fit one lane tile"
    return pl.pallas_call(
        functools.partial(_attn_kernel, scale=HEAD_DIM ** -0.5, heads=heads),
        out_shape=jax.ShapeDtypeStruct((t, w), BF16),
        grid=(batch, n_heads // heads, nq),
        in_specs=[
            pl.BlockSpec((blk, hw), lambda b, g, i: (b * nq + i, g)),
            pl.BlockSpec((seq, hw), lambda b, g, i: (b, g)),
            pl.BlockSpec((seq, hw), lambda b, g, i: (b, g)),
            pl.BlockSpec((1, nq, hw), lambda b, g, i: (b, 0, g)),
        ],
        out_specs=pl.BlockSpec((blk, hw), lambda b, g, i: (b * nq + i, g)),
        compiler_params=_params("parallel", "parallel", "arbitrary"),
        name="attn",
    )(q, k, v, kmean)


def _merge_kernel(cb_ref, cc_ref, cx_ref, pc_ref, px_ref, ga_ref, gb_ref, at_ref, x_ref,
                  cw_ref, bias_ref, wa_ref, wb_ref, wo_ref, ng_ref,
                  x1_ref, xn_ref, xnt_ref, z_ref, *, seq):
    tm = cb_ref.shape[0]
    d = x_ref.shape[1]
    first = (pl.program_id(0) * tm) % seq == 0
    z = cc_ref[...] * cx_ref[...]
    z_prev = jnp.where(first, 0.0, pc_ref[...] * px_ref[...])
    z_ref[0:SUBLANES, :] = z_prev
    z_ref[SUBLANES:, :] = z
    cw = cw_ref[...]
    conv = (cw[2:3, :] * z
            + cw[1:2, :] * z_ref[SUBLANES - 1:SUBLANES - 1 + tm, :]
            + cw[0:1, :] * z_ref[SUBLANES - 2:SUBLANES - 2 + tm, :])
    u = (cb_ref[...] * conv).astype(BF16)
    y_conv = jnp.dot(u, wa_ref[...], preferred_element_type=F32)
    y_attn = jnp.dot(at_ref[...], wb_ref[...], preferred_element_type=F32)
    bias = bias_ref[...]
    merged = (jax.nn.sigmoid(ga_ref[...] + bias[:, :d]) * y_conv
              + jax.nn.sigmoid(gb_ref[...] + bias[:, d:]) * y_attn)
    x1 = x_ref[...] + jnp.dot(merged.astype(BF16), wo_ref[...], preferred_element_type=F32)
    x1_ref[...] = x1
    xn = _rms_norm(x1, ng_ref[...])
    xn_ref[...] = xn.astype(BF16)
    xnt_ref[...] = xn.T.astype(BF16)


def _merge(proj, attn, x, conv_w, gate_bias, wa, wb, wo, ng, seq, tm):
    t, d = x.shape
    cw = wa.shape[0]
    aw = wb.shape[0]
    gcol = (3 * cw + 3 * aw) // d
    halo = lambda c: pl.BlockSpec(
        (SUBLANES, cw), lambda i: (jnp.maximum(i * (tm // SUBLANES) - 1, 0), c))
    return pl.pallas_call(
        functools.partial(_merge_kernel, seq=seq),
        out_shape=(jax.ShapeDtypeStruct((t, d), F32), jax.ShapeDtypeStruct((t, d), BF16),
                   jax.ShapeDtypeStruct((d, t), BF16)),
        grid=(t // tm,),
        in_specs=[
            pl.BlockSpec((tm, cw), lambda i: (i, 0)),
            pl.BlockSpec((tm, cw), lambda i: (i, 1)),
            pl.BlockSpec((tm, cw), lambda i: (i, 2)),
            halo(1), halo(2),
            pl.BlockSpec((tm, d), lambda i: (i, gcol)),
            pl.BlockSpec((tm, d), lambda i: (i, gcol + 1)),
            pl.BlockSpec((tm, aw), lambda i: (i, 0)),
            pl.BlockSpec((tm, d), lambda i: (i, 0)),
            _resident(conv_w.shape), _resident(gate_bias.shape),
            _resident(wa.shape), _resident(wb.shape), _resident(wo.shape), _resident(ng.shape),
        ],
        out_specs=(pl.BlockSpec((tm, d), lambda i: (i, 0)), pl.BlockSpec((tm, d), lambda i: (i, 0)),
                   pl.BlockSpec((d, tm), lambda i: (0, i))),
        scratch_shapes=[pltpu.VMEM((tm + SUBLANES, cw), F32)],
        compiler_params=_params("parallel"),
        name="merge",
    )(proj, proj, proj, proj, proj, proj, proj, attn, x, conv_w, gate_bias, wa, wb, wo, ng)


def _sort_network(n):
    pairs = []

    def merge(lo, hi, r):
        step = r * 2
        if step < hi - lo:
            merge(lo, hi, step)
            merge(lo + r, hi, step)
            pairs.extend((i, i + r) for i in range(lo + r, hi - r, step))
        else:
            pairs.append((lo, lo + r))

    def sort(lo, hi):
        if hi - lo >= 1:
            mid = lo + (hi - lo) // 2
            sort(lo, mid)
            sort(mid + 1, hi)
            merge(lo, hi, 1)

    sort(0, n - 1)
    return pairs


def _compare_exchange(xs, i, j):
    a, b = xs[i], xs[j]
    if b is None:
        return
    if a is None:
        xs[i], xs[j] = b, None
        return
    xs[i], xs[j] = jnp.maximum(a, b), jnp.minimum(a, b)


def _sort_desc(xs):
    n = pl.next_power_of_2(len(xs))
    xs = list(xs) + [None] * (n - len(xs))
    for i, j in _sort_network(n):
        _compare_exchange(xs, i, j)
    return xs


def _top_per_column(s):
    n = s.shape[0] // SUBLANES
    xs = _sort_desc([s[SUBLANES * k:SUBLANES * (k + 1), :] for k in range(n)])
    r = SUBLANES // 2
    while r >= 1:
        xs = [jnp.maximum(xs[k], pltpu.roll(xs[n - 1 - k], SUBLANES - r, 0)) for k in range(n)]
        d = n // 2
        while d >= 1:
            for k in range(n):
                if k & d == 0:
                    _compare_exchange(xs, k, k + d)
            d //= 2
        r //= 2
    return xs


def _pscore_kernel(xn_ref, wq_ref, keys_ref, c1_ref, s2_ref, w1_ref, w2_ref, *, n_heads):
    nkeys, half = keys_ref.shape[1], keys_ref.shape[2]
    tm = xn_ref.shape[0]
    topk = PEER_TOPK
    qp = jnp.dot(xn_ref[...], wq_ref[...], preferred_element_type=F32)
    nt = (((1,), (1,)), ((), ()))
    sub = lax.broadcasted_iota(jnp.int32, (SUBLANES, tm), 0)

    tops = [[jnp.zeros((SUBLANES, tm), F32)] * (topk + 1) for _ in range(2)]
    for h in range(n_heads):
        for p, ref in enumerate((c1_ref, s2_ref)):
            c = (2 * h + p) * half
            s = _dot_nt_3pass(keys_ref[2 * h + p], qp[:, c:c + half])
            ref[h] = s
            col = _top_per_column(s)
            below = jnp.max(jnp.where(s < col[topk - 1][0:1, :], s, -jnp.inf), axis=0, keepdims=True)
            col = [pltpu.roll(x, h, 0) if h else x for x in col] + [below]
            tops[p] = [jnp.where(sub == h, col[k], tops[p][k]) for k in range(topk + 1)]
    a, b = tops

    cand = [a[p] + b[q] for p in range(topk + 1) for q in range(topk + 1)
            if (p + 1) * (q + 1) <= topk + 1]
    best = _sort_desc(cand)[:topk + 1]
    z = best[0] * 0.0
    for k in range(topk):
        z = z + jnp.exp(best[k] - best[0])
    tau = 0.5 * (best[topk - 1] + best[topk])
    half_inv_z = (0.5 / _GELU_C) / z

    for h in range(n_heads):
        row = slice(h, h + 1)
        s1 = c1_ref[h]
        w1_ref[h] = jnp.exp(s1 - a[0][row, :]) * half_inv_z[row, :]
        w2_ref[h] = jnp.exp(s2_ref[h] - b[0][row, :]).astype(w2_ref.dtype)
        c1_ref[h] = tau[row, :] - s1


def _pscore(xn, wq, keys, n_heads, tm):
    t, d = xn.shape
    nkeys = keys.shape[1]
    assert n_heads == SUBLANES and nkeys == SUBLANES * PEER_TOPK
    big = jax.ShapeDtypeStruct((n_heads, nkeys, t), F32)
    bspec = pl.BlockSpec((n_heads, nkeys, tm), lambda i: (0, 0, i))
    return pl.pallas_call(
        functools.partial(_pscore_kernel, n_heads=n_heads),
        out_shape=(big, big, big, jax.ShapeDtypeStruct(big.shape, BF16)),
        grid=(t // tm,),
        in_specs=[pl.BlockSpec((tm, d), lambda i: (i, 0)), _resident(wq.shape), _resident(keys.shape)],
        out_specs=(bspec, bspec, bspec, bspec),
        compiler_params=_params("parallel"),
        name="pscore",
    )(xn, wq, keys)


_GELU_C = 0.7978845608028654
_GELU_K = (_GELU_C * 0.044715) ** (1.0 / 3.0)
_ACT_ROWS = 32
_GATE_ROWS = 32


def _pdense_kernel(xnt_ref, u_ref, v_ref, c1_ref, s2_ref, w1_ref, w2_ref, x1_ref, g_ref,
                   o_ref, gate_ref, a_ref, act_ref, *, n_heads):
    e = pl.program_id(1)
    n_chunks = pl.num_programs(1) - 1
    nkeys = s2_ref.shape[1]
    ec = u_ref.shape[0]
    tm = s2_ref.shape[2]
    tn = (((0,), (0,)), ((), ()))

    @pl.when(e == 0)
    def _():
        o_ref[...] = jnp.zeros_like(o_ref)
        act_ref[1] = jnp.zeros(act_ref.shape[1:], act_ref.dtype)

    groups = ec // nkeys
    slot = e % 2
    chunk = jnp.minimum(e, n_chunks - 1)
    row0 = (chunk * groups) % SUBLANES

    def gate_groups(g0, g1):
        gis = range(g0, g1)
        w1_rows = {gi: [w1_ref[h, pl.ds(row0 + gi, 1), :] for h in range(n_heads)] for gi in gis}
        c1_rows = {gi: [c1_ref[h, pl.ds(row0 + gi, 1), :] for h in range(n_heads)] for gi in gis}
        for lt in range(tm // LANES):
            ln = slice(lt * LANES, (lt + 1) * LANES)
            for r0 in range(0, nkeys, _GATE_ROWS):
                rs = slice(r0, r0 + _GATE_ROWS)
                accs = {gi: jnp.zeros((_GATE_ROWS, LANES), BF16) for gi in gis}
                for h in range(n_heads):
                    s2t = s2_ref[h, rs, ln]
                    w2t = w2_ref[h, rs, ln]
                    for gi in gis:
                        w = w1_rows[gi][h][:, ln].astype(BF16) * w2t
                        accs[gi] = accs[gi] + jnp.where(s2t >= c1_rows[gi][h][:, ln], w, jnp.zeros_like(w))
                for gi in gis:
                    gate_ref[gi * nkeys + r0:gi * nkeys + r0 + _GATE_ROWS, ln] = accs[gi]

    def act_rows(r0, r1):
        for r in range(r0, r1, _ACT_ROWS):
            rows = slice(r, r + _ACT_ROWS)
            a = a_ref[rows, :]
            p = (a * _GELU_C).astype(BF16)
            q = (a * _GELU_K).astype(BF16)
            th = jnp.tanh(p + q * q * q)
            act_ref[slot, rows, :] = (p + p * th) * gate_ref[rows, :]

    halves = 2
    for m in range(halves):
        rows = slice(m * ec // halves, (m + 1) * ec // halves)
        a_ref[rows, :] = jnp.dot(u_ref[rows, :], xnt_ref[...], preferred_element_type=F32)
        gate_groups(m * groups // halves, (m + 1) * groups // halves)
    o_ref[...] += lax.dot_general(act_ref[1 - slot], v_ref[...], tn, preferred_element_type=F32)
    act_rows(0, ec)

    @pl.when(e == n_chunks)
    def _():
        o_ref[...] = _rms_norm(x1_ref[...] + o_ref[...], g_ref[...])


def _pdense(xnt, u, v, c1, s2, w1, w2, x1, g, tm, ec):
    d, t = xnt.shape
    n_chunks = u.shape[0] // ec
    n_heads, nkeys, _ = s2.shape
    groups = ec // nkeys
    assert SUBLANES % groups == 0
    rows4 = pl.BlockSpec((n_heads, SUBLANES, tm),
                         lambda i, e: (0, jnp.minimum(e, n_chunks - 1) * groups // SUBLANES, i))
    once = pl.Buffered(1)
    return pl.pallas_call(
        functools.partial(_pdense_kernel, n_heads=n_heads),
        out_shape=jax.ShapeDtypeStruct((t, d), F32),
        grid=(t // tm, n_chunks + 1),
        in_specs=[
            pl.BlockSpec((d, tm), lambda i, e: (0, i), pipeline_mode=once),
            pl.BlockSpec((ec, d), lambda i, e: (jnp.minimum(e, n_chunks - 1), 0)),
            pl.BlockSpec((ec, d), lambda i, e: (jnp.maximum(e - 1, 0), 0)),
            rows4,
            pl.BlockSpec((n_heads, nkeys, tm), lambda i, e: (0, 0, i)),
            rows4,
            pl.BlockSpec((n_heads, nkeys, tm), lambda i, e: (0, 0, i)),
            pl.BlockSpec((tm, d), lambda i, e: (i, 0), pipeline_mode=once),
            pl.BlockSpec((1, d), lambda i, e: (0, 0)),
        ],
        out_specs=pl.BlockSpec((tm, d), lambda i, e: (i, 0), pipeline_mode=once),
        scratch_shapes=[pltpu.VMEM((ec, tm), BF16), pltpu.VMEM((ec, tm), F32),
                        pltpu.VMEM((2, ec, tm), BF16)],
        compiler_params=_params("parallel", "arbitrary", vmem_limit_bytes=BIG_TILE_VMEM_LIMIT_BYTES),
        name="pdense",
    )(xnt, u, v, c1, s2, w1, w2, x1, g)


def _layer(x, pos, an_g, w_in, gate_bias, conv_w, wa, wb, wo, fn_g, wq, keys, pu, pv, out_g, *,
           batch, seq, tiles):
    cw = wa.shape[0]
    aw = wb.shape[0]
    n_heads = aw // HEAD_DIM
    p_heads = keys.shape[0]

    inv_freq = 1.0 / (ROPE_THETA ** (jnp.arange(0, ROT_DIM, 2, dtype=F32) / ROT_DIM))
    invf = jnp.concatenate([inv_freq, inv_freq, jnp.zeros((HEAD_DIM - ROT_DIM,), F32)])[None, :]

    proj, q, k, v, kmean = _inproj(x, an_g[None, :], w_in.astype(BF16), pos, invf,
                                   tiles["inproj_tm"], aw, 3 * cw, n_heads)
    kmean = kmean.reshape(batch, seq // MOBA_BLOCK, aw)
    attn = _attn(q, k, v, kmean, batch, seq, n_heads, min(tiles["attn_heads"], n_heads))
    x1, xn, xnt = _merge(proj, attn, x, conv_w, gate_bias[None, :], wa.astype(BF16), wb.astype(BF16),
                    wo.astype(BF16), fn_g[None, :], seq, tiles["merge_tm"])
    keys2 = keys.reshape(p_heads * 2, keys.shape[2], keys.shape[3])
    c1, s2, w1, w2 = _pscore(xn, wq.astype(BF16), keys2, p_heads, tiles["pscore_tm"])
    return _pdense(xnt, pu.astype(BF16), pv.astype(BF16), c1, s2, w1, w2, x1, out_g[None, :],
                   tiles["pdense_tm"], tiles["pdense_ec"])


def _tiles(t, seq):
    return dict(
        inproj_tm=min(1024, t),
        attn_heads=8,
        merge_tm=min(256, seq),
        pscore_tm=min(512, t),
        pdense_tm=min(1024, t), pdense_ec=512,
    )


def kernel(x, positions, attn_norm_g, w_in, gate_bias, conv_w, w_branch_conv, w_branch_attn, w_out,
           ffn_norm_g, w_peer_query, peer_sub_keys, peer_u, peer_v, final_norm_g):
    b, s, d = x.shape
    depth = w_in.shape[0]
    assert depth == 1, "final norm is fused into the last layer's kernel"
    t = b * s
    xt = x.reshape(t, d)
    pos = positions.reshape(t, 1)
    out = _layer(xt, pos, attn_norm_g[0], w_in[0], gate_bias[0], conv_w[0], w_branch_conv[0],
                 w_branch_attn[0], w_out[0], ffn_norm_g[0], w_peer_query[0], peer_sub_keys[0],
                 peer_u[0], peer_v[0], final_norm_g, batch=b, seq=s, tiles=_tiles(t, s))
    return out.reshape(b, s, d)
```
